```python
import math
import jax, jax.numpy as jnp
from jax import lax
import numpy as np

D_MODEL = 1024
BATCH = 16
SEQ = 4096
DEPTH = 1
DEC_BATCH = 128
DEC_SEQ = 4
PAST_LEN = 8192
PAGE_SIZE = 128

RET_HEADS = 4
RET_DK = 128
RET_DV = 256
RET_CHUNK = 128
DIFF_HEADS = 4
DIFF_DK = 64
DIFF_DV = 2 * DIFF_DK
Q_BLOCK = 128
N_EXPERTS = 256
TOP_K = 8
N_GROUPS = 8
TOPK_GROUPS = 4
D_EXPERT = 256
D_SHARED = 256
ROUTED_SCALE = 2.5
MOE_BLOCK = 128
P_DIM = 256
LN_EPS = 1e-5
DEEPNORM_ALPHA = (2 * DEPTH) ** 0.25
DEEPNORM_BETA = (8 * DEPTH) ** -0.25

RET_QK_W = RET_HEADS * RET_DK
RET_V_W = RET_HEADS * RET_DV
DIFF_QK_W = DIFF_HEADS * 2 * DIFF_DK
DIFF_V_W = DIFF_HEADS * DIFF_DV
SPLITS = (RET_QK_W, RET_QK_W, RET_V_W, RET_V_W, DIFF_QK_W, DIFF_QK_W, DIFF_V_W, D_MODEL, D_MODEL)
N_IN = sum(SPLITS)

kernel_name = 'hybrid_retention_diffattn_moe_step'


def layer_norm(x, w, b):
    xf = x.astype(jnp.float32)
    mu = xf.mean(-1, keepdims=True)
    var = jnp.square(xf - mu).mean(-1, keepdims=True)
    return ((xf - mu) * lax.rsqrt(var + LN_EPS) * w + b).astype(x.dtype)


def project_in(x, w_in):
    z = x @ w_in
    return jnp.split(z, [int(o) for o in np.cumsum(SPLITS)[:-1]], axis=-1)


def retention_log_decay():
    return jnp.log1p(-(2.0 ** (-5.0 - jnp.arange(RET_HEADS, dtype=jnp.float32))))


def rotate(x, pos):
    inv = 1.0 / (10000.0 ** jnp.linspace(0.0, 1.0, RET_DK // 2))
    ang = pos.astype(jnp.float32)[:, None] * inv[None, :]
    cos = jnp.cos(ang)[None, :, None, :]
    sin = jnp.sin(ang)[None, :, None, :]
    x1 = x[..., 0::2]
    x2 = x[..., 1::2]
    out = jnp.stack([x1 * cos - x2 * sin, x2 * cos + x1 * sin], axis=-1)
    return out.reshape(x.shape).astype(x.dtype)


def retention_heads(r_q, r_k, r_v, pos):
    B, L, _ = r_q.shape
    q = rotate(r_q.reshape(B, L, RET_HEADS, RET_DK), pos)
    k = rotate(r_k.reshape(B, L, RET_HEADS, RET_DK), pos) * (RET_DK ** -0.5)
    v = r_v.reshape(B, L, RET_HEADS, RET_DV)
    return q.transpose(0, 2, 1, 3), k.transpose(0, 2, 1, 3), v.transpose(0, 2, 1, 3)


def retention_chunk(state, q, k, v, log_gamma):
    L = q.shape[2]
    idx = jnp.arange(L, dtype=jnp.float32)
    rel = idx[:, None] - idx[None, :]
    lg = log_gamma[:, None, None]
    decay = jnp.where(rel >= 0, jnp.exp(jnp.maximum(rel, 0.0) * lg), 0.0)
    qk = jnp.einsum('bhld,bhmd->bhlm', q, k) * decay
    o = jnp.einsum('bhlm,bhme->bhle', qk, v)
    q_dec = q * jnp.exp((idx + 1.0)[None, :, None] * lg)
    o = o + jnp.einsum('bhld,bhde->bhle', q_dec, state)
    k_dec = k * jnp.exp((L - 1.0 - idx)[None, :, None] * lg)
    new_state = jnp.exp(L * log_gamma)[:, None, None] * state + jnp.einsum('bhld,bhle->bhde', k_dec, v)
    return new_state.astype(state.dtype), o


def retention_prompt(q, k, v, log_gamma):
    B, H, S, _ = q.shape
    n = S // RET_CHUNK

    def chunks(t):
        return t.reshape(B, H, n, RET_CHUNK, t.shape[-1]).transpose(2, 0, 1, 3, 4)

    def step(st, qkv):
        qc, kc, vc = qkv
        return retention_chunk(st, qc, kc, vc, log_gamma)

    state0 = jnp.zeros((B, H, RET_DK, RET_DV), jnp.float32)
    state, o = lax.scan(step, state0, (chunks(q), chunks(k), chunks(v)))
    return state, o.transpose(1, 2, 0, 3, 4).reshape(B, H, S, RET_DV)


def retention_out(o, r_g, norm_w):
    of = o.transpose(0, 2, 1, 3).astype(jnp.float32)
    mu = of.mean(-1, keepdims=True)
    var = jnp.square(of - mu).mean(-1, keepdims=True)
    of = (of - mu) * lax.rsqrt(var + LN_EPS)
    B, L = of.shape[:2]
    of = of.reshape(B, L, RET_V_W) * norm_w
    return (jax.nn.silu(r_g) * of).astype(r_g.dtype)


def diff_lambda(lq1, lk1, lq2, lk2, lambda_init):
    f = jnp.float32
    return (jnp.exp(jnp.sum(lq1.astype(f) * lk1.astype(f)))
            - jnp.exp(jnp.sum(lq2.astype(f) * lk2.astype(f))) + lambda_init)


def diff_attend(q, k, v, qpos, kpos, lam):
    s = jnp.einsum('bqhcd,bkhcd->bhcqk', q, k).astype(jnp.float32) * (DIFF_DK ** -0.5)
    mask = kpos[None, :] <= qpos[:, None]
    a = jax.nn.softmax(jnp.where(mask, s, -jnp.inf), axis=-1)
    w = a[:, :, 0] - lam * a[:, :, 1]
    return jnp.einsum('bhqk,bkhe->bqhe', w.astype(v.dtype), v)


def diff_prompt(q, k, v, lam):
    B, S = q.shape[:2]
    nb = S // Q_BLOCK
    qb = q.reshape(B, nb, Q_BLOCK, DIFF_HEADS, 2, DIFF_DK).swapaxes(0, 1)
    kpos = jnp.arange(S)

    def block(args):
        qi, i = args
        return diff_attend(qi, k, v, i * Q_BLOCK + jnp.arange(Q_BLOCK), kpos, lam)

    o = lax.map(block, (qb, jnp.arange(nb)))
    return o.swapaxes(0, 1).reshape(B, S, DIFF_HEADS, DIFF_DV)


def diff_out(o, norm_w, lambda_init):
    of = o.astype(jnp.float32)
    of = of * lax.rsqrt(jnp.mean(jnp.square(of), -1, keepdims=True) + LN_EPS) * norm_w * (1.0 - lambda_init)
    B, L = o.shape[:2]
    return of.reshape(B, L, DIFF_V_W).astype(o.dtype)


def route(h, router_w, router_b):
    T = h.shape[0]
    per = N_EXPERTS // N_GROUPS
    s = jax.nn.sigmoid((h @ router_w).astype(jnp.float32))
    sb = s + router_b.astype(jnp.float32)
    gscore = lax.top_k(sb.reshape(T, N_GROUPS, per), 2)[0].sum(-1)
    _, gidx = lax.top_k(gscore, TOPK_GROUPS)
    gmask = (gidx[:, :, None] == jnp.arange(N_GROUPS)[None, None, :]).any(axis=1)
    emask = jnp.repeat(gmask, per, axis=1)
    _, eidx = lax.top_k(jnp.where(emask, sb, -jnp.inf), TOP_K)
    gate = jnp.take_along_axis(s, eidx, axis=1)
    gate = gate / gate.sum(-1, keepdims=True) * ROUTED_SCALE
    return eidx, gate


def routed_experts(h, eidx, gate, w1, w3, w2):
    T = h.shape[0]
    A = T * TOP_K
    flat_e = eidx.reshape(A)
    order = jnp.argsort(flat_e)
    e_sorted = flat_e[order]
    counts = jnp.bincount(flat_e, length=N_EXPERTS)
    padded = (counts + MOE_BLOCK - 1) // MOE_BLOCK * MOE_BLOCK
    seg_start = jnp.cumsum(counts) - counts
    pad_end = jnp.cumsum(padded)
    pad_start = pad_end - padded
    dest = pad_start[e_sorted] + jnp.arange(A) - seg_start[e_sorted]
    n_blocks = (A + N_EXPERTS * (MOE_BLOCK - 1) + MOE_BLOCK - 1) // MOE_BLOCK
    R = n_blocks * MOE_BLOCK
    buf_tok = jnp.zeros((R,), jnp.int32).at[dest].set((order // TOP_K).astype(jnp.int32))
    buf_w = jnp.zeros((R,), jnp.float32).at[dest].set(gate.reshape(A)[order])
    block_expert = jnp.minimum(
        jnp.searchsorted(pad_end, jnp.arange(n_blocks) * MOE_BLOCK, side='right'), N_EXPERTS - 1)

    def run(args):
        tok_b, w_b, e = args
        xb = h[tok_b]
        y = (jax.nn.silu(xb @ w1[e]) * (xb @ w3[e])) @ w2[e]
        return y * w_b[:, None].astype(y.dtype)

    y = lax.map(run, (buf_tok.reshape(n_blocks, MOE_BLOCK), buf_w.reshape(n_blocks, MOE_BLOCK), block_expert))
    return jnp.zeros_like(h).at[buf_tok].add(y.reshape(R, h.shape[1]))


def layer_tail(x, o_r, r_g, o_d, g_ret, g_diff, p, lambda_init,
               ret_norm_w, diff_norm_w, w_up_ret, w_up_diff, w_out, ln1_w, ln1_b,
               router_w, router_b, ew1, ew3, ew2, sw1, sw3, sw2, ln2_w, ln2_b, w_pe, w_pg):
    o_ret = retention_out(o_r, r_g, ret_norm_w)
    o_diff = diff_out(o_d, diff_norm_w, lambda_init)
    u = jax.nn.sigmoid(g_ret) * (o_ret @ w_up_ret) + jax.nn.sigmoid(g_diff) * (o_diff @ w_up_diff)
    h = layer_norm(DEEPNORM_ALPHA * x + u @ w_out, ln1_w, ln1_b)
    B, L, D = h.shape
    hf = h.reshape(B * L, D)
    eidx, gate = route(hf, router_w, router_b)
    shared = (jax.nn.silu(hf @ sw1) * (hf @ sw3)) @ sw2
    m = (routed_experts(hf, eidx, gate, ew1, ew3, ew2) + shared).reshape(B, L, D)
    h = layer_norm(DEEPNORM_ALPHA * h + m, ln2_w, ln2_b)
    return h + jax.nn.sigmoid(h @ w_pg) * (p @ w_pe)


def setup_inputs(seed: int = 0) -> dict:
    key = jax.random.key(seed)
    it = iter(jax.random.split(key, 40))

    def nrm(shape, scale):
        return jax.random.normal(next(it), shape, jnp.float32) * scale

    n_pages = PAST_LEN // PAGE_SIZE
    n_used = DEC_BATCH * n_pages
    n_pool = n_used + max(1, n_used // 4)
    col = np.ones((N_IN,), np.float32)
    offs = np.cumsum((0,) + SPLITS)
    col[offs[2]:offs[3]] = DEEPNORM_BETA
    col[offs[6]:offs[7]] = DEEPNORM_BETA
    D = D_MODEL
    return {
        'x_prompt': nrm((BATCH, SEQ, D), 1.0),
        'x_sample': nrm((DEC_BATCH, DEC_SEQ, D), 1.0),
        'cache_diff_k': nrm((DEPTH, n_pool, PAGE_SIZE, DIFF_HEADS, 2, DIFF_DK), 1.0),
        'cache_diff_v': nrm((DEPTH, n_pool, PAGE_SIZE, DIFF_HEADS, DIFF_DV), 1.0),
        'state_retention': nrm((DEPTH, DEC_BATCH, RET_HEADS, RET_DK, RET_DV), 0.3),
        'page_table': jax.random.permutation(next(it), n_pool)[:n_used].reshape(DEC_BATCH, n_pages).astype(jnp.int32),
        'p_prompt': nrm((DEPTH, BATCH, SEQ, P_DIM), 1.0),
        'p_sample': nrm((DEPTH, DEC_BATCH, DEC_SEQ, P_DIM), 1.0),
        'w_in': nrm((DEPTH, D, N_IN), D ** -0.5) * jnp.asarray(col),
        'ret_norm_w': 1.0 + nrm((DEPTH, RET_V_W), 0.02),
        'diff_lq1': nrm((DEPTH, DIFF_DK), 0.1),
        'diff_lk1': nrm((DEPTH, DIFF_DK), 0.1),
        'diff_lq2': nrm((DEPTH, DIFF_DK), 0.1),
        'diff_lk2': nrm((DEPTH, DIFF_DK), 0.1),
        'diff_norm_w': 1.0 + nrm((DEPTH, DIFF_DV), 0.02),
        'w_up_ret': nrm((DEPTH, RET_V_W, D), RET_V_W ** -0.5),
        'w_up_diff': nrm((DEPTH, DIFF_V_W, D), DIFF_V_W ** -0.5),
        'w_out': nrm((DEPTH, D, D), D ** -0.5 * DEEPNORM_BETA),
        'ln1_w': 1.0 + nrm((DEPTH, D), 0.02),
        'ln1_b': nrm((DEPTH, D), 0.02),
        'router_w': nrm((DEPTH, D, N_EXPERTS), D ** -0.5),
        'router_b': nrm((DEPTH, N_EXPERTS), 0.01),
        'expert_w1': nrm((DEPTH, N_EXPERTS, D, D_EXPERT), D ** -0.5),
        'expert_w3': nrm((DEPTH, N_EXPERTS, D, D_EXPERT), D ** -0.5),
        'expert_w2': nrm((DEPTH, N_EXPERTS, D_EXPERT, D), D_EXPERT ** -0.5 * DEEPNORM_BETA),
        'shared_w1': nrm((DEPTH, D, D_SHARED), D ** -0.5),
        'shared_w3': nrm((DEPTH, D, D_SHARED), D ** -0.5),
        'shared_w2': nrm((DEPTH, D_SHARED, D), D_SHARED ** -0.5 * DEEPNORM_BETA),
        'ln2_w': 1.0 + nrm((DEPTH, D), 0.02),
        'ln2_b': nrm((DEPTH, D), 0.02),
        'w_pe': nrm((DEPTH, P_DIM, D), P_DIM ** -0.5 * DEEPNORM_BETA),
        'w_pg': nrm((DEPTH, D, D), D ** -0.5),
    }


def reference(x_prompt, x_sample, cache_diff_k, cache_diff_v, state_retention, page_table, p_prompt, p_sample,
              w_in, ret_norm_w, diff_lq1, diff_lk1, diff_lq2, diff_lk2, diff_norm_w, w_up_ret, w_up_diff, w_out,
              ln1_w, ln1_b, router_w, router_b, expert_w1, expert_w3, expert_w2, shared_w1, shared_w3, shared_w2,
              ln2_w, ln2_b, w_pe, w_pg):
    log_gamma = retention_log_decay()
    B, S, _ = x_prompt.shape
    Bd, Ls, _ = x_sample.shape
    past = page_table.shape[1] * PAGE_SIZE
    pos_p = jnp.arange(S)
    pos_s = past + jnp.arange(Ls)
    kpos_s = jnp.arange(past + Ls)
    yp, ys = x_prompt, x_sample
    kp_l, vp_l, sp_l, ks_l, vs_l, ss_l = [], [], [], [], [], []
    for i in range(DEPTH):
        lambda_init = 0.8 - 0.6 * math.exp(-0.3 * i)
        lam = diff_lambda(diff_lq1[i], diff_lk1[i], diff_lq2[i], diff_lk2[i], lambda_init)
        wl = (ret_norm_w[i], diff_norm_w[i], w_up_ret[i], w_up_diff[i], w_out[i], ln1_w[i], ln1_b[i],
              router_w[i], router_b[i], expert_w1[i], expert_w3[i], expert_w2[i],
              shared_w1[i], shared_w3[i], shared_w2[i], ln2_w[i], ln2_b[i], w_pe[i], w_pg[i])

        r_q, r_k, r_v, r_g, d_q, d_k, d_v, g_ret, g_diff = project_in(yp, w_in[i])
        q, k, v = retention_heads(r_q, r_k, r_v, pos_p)
        st_p, o_r = retention_prompt(q, k, v, log_gamma)
        kd = d_k.reshape(B, S, DIFF_HEADS, 2, DIFF_DK)
        vd = d_v.reshape(B, S, DIFF_HEADS, DIFF_DV)
        o_d = diff_prompt(d_q.reshape(B, S, DIFF_HEADS, 2, DIFF_DK), kd, vd, lam)
        yp = layer_tail(yp, o_r, r_g, o_d, g_ret, g_diff, p_prompt[i], lambda_init, *wl)
        kp_l.append(kd)
        vp_l.append(vd)
        sp_l.append(st_p)

        r_q, r_k, r_v, r_g, d_q, d_k, d_v, g_ret, g_diff = project_in(ys, w_in[i])
        q, k, v = retention_heads(r_q, r_k, r_v, pos_s)
        st_s, o_r = retention_chunk(state_retention[i], q, k, v, log_gamma)
        kd = d_k.reshape(Bd, Ls, DIFF_HEADS, 2, DIFF_DK)
        vd = d_v.reshape(Bd, Ls, DIFF_HEADS, DIFF_DV)
        k_all = jnp.concatenate(
            [cache_diff_k[i][page_table].reshape(Bd, past, DIFF_HEADS, 2, DIFF_DK), kd], axis=1)
        v_all = jnp.concatenate(
            [cache_diff_v[i][page_table].reshape(Bd, past, DIFF_HEADS, DIFF_DV), vd], axis=1)
        o_d = diff_attend(d_q.reshape(Bd, Ls, DIFF_HEADS, 2, DIFF_DK), k_all, v_all, pos_s, kpos_s, lam)
        ys = layer_tail(ys, o_r, r_g, o_d, g_ret, g_diff, p_sample[i], lambda_init, *wl)
        ks_l.append(kd)
        vs_l.append(vd)
        ss_l.append(st_s)
    return (yp, ys, jnp.stack(kp_l), jnp.stack(vp_l), jnp.stack(sp_l), jnp.stack(ks_l), jnp.stack(vs_l), jnp.stack(ss_l))
```

```python
import functools
import math

import jax
import jax.numpy as jnp
import numpy as np
from jax import lax
from jax.experimental import pallas as pl
from jax.experimental.pallas import tpu as pltpu

F32 = jnp.float32
BF16 = jnp.bfloat16

D_MODEL = 1024
PAGE_SIZE = 128
RET_HEADS = 4
RET_DK = 128
RET_DV = 256
RET_CHUNK = 128
DIFF_HEADS = 4
DIFF_DK = 64
DIFF_DV = 128
N_EXPERTS = 256
TOP_K = 8
N_GROUPS = 8
TOPK_GROUPS = 4
D_EXPERT = 256
ROUTED_SCALE = 2.5
LN_EPS = 1e-5

RET_QK_W = RET_HEADS * RET_DK
RET_V_W = RET_HEADS * RET_DV
DIFF_QK_W = DIFF_HEADS * 2 * DIFF_DK
DIFF_V_W = DIFF_HEADS * DIFF_DV
_SPLITS = (RET_QK_W, RET_QK_W, RET_V_W, RET_V_W, DIFF_QK_W, DIFF_QK_W, DIFF_V_W, D_MODEL, D_MODEL)
_OFFS = tuple(int(o) for o in np.cumsum((0,) + _SPLITS))

LANES = 128
VMEM_LIMIT = 56 * 1024 * 1024

MOE_BLOCK = 256
PAGES_PER_STEP = 8


def _cparams(sem, vmem=VMEM_LIMIT):
    return pltpu.CompilerParams(dimension_semantics=sem, vmem_limit_bytes=vmem)


def _const_spec(shape):
    nd = len(shape)
    return pl.BlockSpec(shape, lambda *_: (0,) * nd)


def _sigmoid(x):
    return 1.0 / (1.0 + jnp.exp(-x))


def _silu(x):
    return x * _sigmoid(x)


def _dot(a, b):
    return jnp.dot(a, b, preferred_element_type=F32)


def _dot_nt(a, b):
    return lax.dot_general(a, b, (((1,), (1,)), ((), ())), preferred_element_type=F32)


def _dot_tn(a, b):
    return lax.dot_general(a, b, (((0,), (0,)), ((), ())), preferred_element_type=F32)


def _swap_pairs(x):
    lane = lax.broadcasted_iota(jnp.int32, x.shape, 1)
    nxt = pltpu.roll(x, LANES - 1, 1)
    prv = pltpu.roll(x, 1, 1)
    return jnp.where((lane & 1) == 0, nxt, prv)


def _proj_kernel(x_ref, w_ref, cos_ref, sin_ref,
                 rq_ref, rk_ref, rv_ref, dq_ref, dk_ref, dv_ref, dkb_ref, dvb_ref):
    xb = x_ref[...].astype(BF16)
    cos = cos_ref[...]
    sin = sin_ref[...]

    def mm(lo, hi):
        return _dot(xb, w_ref[:, lo:hi])

    q = mm(0, 512)
    k = mm(512, 1024)
    for h in range(RET_HEADS):
        sl = slice(h * RET_DK, (h + 1) * RET_DK)
        qh = q[:, sl]
        kh = k[:, sl]
        rq_ref[:, sl] = (qh * cos + _swap_pairs(qh) * sin).astype(BF16)
        rk_ref[:, sl] = ((kh * cos + _swap_pairs(kh) * sin) * (RET_DK ** -0.5)).astype(BF16)
    rv_ref[...] = mm(1024, 2048).astype(BF16)
    dq_ref[...] = (mm(2048, 2560) * (DIFF_DK ** -0.5)).astype(BF16)
    dk = mm(2560, 3072)
    dk_ref[...] = dk
    dkb_ref[...] = dk.astype(BF16)
    dv = mm(3072, 3584)
    dv_ref[...] = dv
    dvb_ref[...] = dv.astype(BF16)


def _proj(x2d, w_mix, cos_t, sin_t, tm):
    t = x2d.shape[0]
    nt = cos_t.shape[0] // tm
    row = lambda w: pl.BlockSpec((tm, w), lambda i: (i, 0))
    tab = pl.BlockSpec((tm, LANES), lambda i: (i % nt, 0))
    out_shapes = (
        jax.ShapeDtypeStruct((t, RET_QK_W), BF16), jax.ShapeDtypeStruct((t, RET_QK_W), BF16),
        jax.ShapeDtypeStruct((t, RET_V_W), BF16), jax.ShapeDtypeStruct((t, DIFF_QK_W), BF16),
        jax.ShapeDtypeStruct((t, DIFF_QK_W), F32), jax.ShapeDtypeStruct((t, DIFF_V_W), F32),
        jax.ShapeDtypeStruct((t, DIFF_QK_W), BF16), jax.ShapeDtypeStruct((t, DIFF_V_W), BF16),
    )
    return pl.pallas_call(
        _proj_kernel,
        out_shape=out_shapes,
        grid=(t // tm,),
        in_specs=[row(D_MODEL), _const_spec(w_mix.shape), tab, tab],
        out_specs=(row(512), row(512), row(1024), row(512), row(512), row(512), row(512), row(512)),
        compiler_params=_cparams(("parallel",)),
        name="proj",
    )(x2d, w_mix, cos_t, sin_t)


def _rotation_tables(pos):
    inv = 1.0 / (10000.0 ** jnp.linspace(0.0, 1.0, RET_DK // 2))
    ang = pos.astype(F32)[:, None] * inv[None, :]
    cos = jnp.repeat(jnp.cos(ang), 2, axis=1)
    sin = jnp.sin(ang)
    sin = jnp.stack([-sin, sin], axis=-1).reshape(pos.shape[0], RET_DK)
    return cos, sin


def _retention_kernel(q_ref, k_ref, v_ref, s0_ref, dm_ref, qd_ref, kd_ref, o_ref, s_ref, st_ref,
                      *, n_sub, gl):
    c = pl.program_id(1)

    @pl.when(c == 0)
    def _():
        st_ref[...] = s0_ref[0]

    for j in range(n_sub):
        rows = slice(j * RET_CHUNK, (j + 1) * RET_CHUNK)
        for h in range(RET_HEADS):
            q = q_ref[0, rows, h * RET_DK:(h + 1) * RET_DK]
            k = k_ref[0, rows, h * RET_DK:(h + 1) * RET_DK]
            v = v_ref[0, rows, h * RET_DV:(h + 1) * RET_DV]
            st = st_ref[h]
            qk = _dot_nt(q, k) * dm_ref[h]
            q_dec = (q.astype(F32) * qd_ref[h]).astype(BF16)
            o = _dot(qk.astype(BF16), v) + _dot(q_dec, st.astype(BF16))
            k_dec = (k.astype(F32) * kd_ref[h]).astype(BF16)
            st_ref[h] = gl[h] * st + _dot_tn(k_dec, v)
            mu = jnp.mean(o, axis=-1, keepdims=True)
            oc = o - mu
            var = jnp.mean(oc * oc, axis=-1, keepdims=True)
            o_ref[0, rows, h * RET_DV:(h + 1) * RET_DV] = (oc * lax.rsqrt(var + LN_EPS)).astype(BF16)

    @pl.when(c == pl.num_programs(1) - 1)
    def _():
        s_ref[0] = st_ref[...]


def _retention_tables(length):
    lg = np.log1p(-(2.0 ** (-5.0 - np.arange(RET_HEADS, dtype=np.float64))))
    idx = np.arange(RET_CHUNK, dtype=np.float64)
    rel = idx[:, None] - idx[None, :]
    valid = (idx < length)
    dm = np.where((rel >= 0) & valid[:, None] & valid[None, :],
                  np.exp(np.maximum(rel, 0.0)[None] * lg[:, None, None]), 0.0)
    qd = np.exp((idx + 1.0)[None, :] * lg[:, None])
    kd = np.where(valid[None, :], np.exp((length - 1.0 - idx)[None, :] * lg[:, None]), 0.0)
    bc = lambda a: np.ascontiguousarray(np.broadcast_to(a[:, :, None], (RET_HEADS, RET_CHUNK, RET_DK)))
    gl = tuple(float(np.exp(length * g)) for g in lg)
    return (jnp.asarray(dm, F32), jnp.asarray(bc(qd), F32), jnp.asarray(bc(kd), F32)), gl


def _retention(rq, rk, rv, state0, length, lb):
    b, s, _ = rq.shape
    (dm, qd, kd), gl = _retention_tables(length)
    seq = lambda w: pl.BlockSpec((1, lb, w), lambda i, c: (i, c, 0))
    st_spec = pl.BlockSpec((1, RET_HEADS, RET_DK, RET_DV), lambda i, c: (i, 0, 0, 0))
    return pl.pallas_call(
        functools.partial(_retention_kernel, n_sub=lb // RET_CHUNK, gl=gl),
        out_shape=(jax.ShapeDtypeStruct((b, s, RET_V_W), BF16),
                   jax.ShapeDtypeStruct((b, RET_HEADS, RET_DK, RET_DV), F32)),
        grid=(b, s // lb),
        in_specs=[seq(RET_QK_W), seq(RET_QK_W), seq(RET_V_W), st_spec,
                  _const_spec(dm.shape), _const_spec(qd.shape), _const_spec(kd.shape)],
        out_specs=(seq(RET_V_W), st_spec),
        scratch_shapes=[pltpu.VMEM((RET_HEADS, RET_DK, RET_DV), F32)],
        compiler_params=_cparams(("parallel", "arbitrary")),
        name="retention",
    )(rq, rk, rv, state0, dm, qd, kd)


def _diff_lambda(lq1_ref, lk1_ref, lq2_ref, lk2_ref, lambda_init):
    a = jnp.sum(lq1_ref[...] * lk1_ref[...], axis=-1, keepdims=True)
    b = jnp.sum(lq2_ref[...] * lk2_ref[...], axis=-1, keepdims=True)
    return jnp.exp(a) - jnp.exp(b) + lambda_init


def _rms_head(o, nw, lambda_init):
    ms = jnp.mean(o * o, axis=-1, keepdims=True)
    return o * lax.rsqrt(ms + LN_EPS) * nw * (1.0 - lambda_init)


def _diffattn_kernel(q_ref, k_ref, v_ref, lq1_ref, lk1_ref, lq2_ref, lk2_ref, nw_ref, o_ref,
                     *, bq, lambda_init):
    qi = pl.program_id(2)
    q = q_ref[0]
    lane = lax.broadcasted_iota(jnp.int32, q.shape, 1)
    zero = jnp.zeros_like(q)
    qq = jnp.concatenate([jnp.where(lane < DIFF_DK, q, zero), jnp.where(lane >= DIFF_DK, q, zero)], axis=0)

    def step(j, carry, masked):
        m, l, acc = carry
        off = pl.multiple_of(j * bq, bq)
        kb = k_ref[0, pl.ds(off, bq), :]
        vb = v_ref[0, pl.ds(off, bq), :]
        s = _dot_nt(qq, kb)
        if masked:
            r = lax.broadcasted_iota(jnp.int32, s.shape, 0)
            r = jnp.where(r >= bq, r - bq, r)
            cidx = lax.broadcasted_iota(jnp.int32, s.shape, 1)
            s = jnp.where(cidx <= r, s, -jnp.inf)
        m_new = jnp.maximum(m, jnp.max(s, axis=-1, keepdims=True))
        p = jnp.exp(s - m_new)
        alpha = jnp.exp(m - m_new)
        l = alpha * l + jnp.sum(p, axis=-1, keepdims=True)
        acc = alpha * acc + _dot(p.astype(BF16), vb)
        return m_new, l, acc

    init = (jnp.full((2 * bq, 1), -jnp.inf, F32), jnp.zeros((2 * bq, 1), F32),
            jnp.zeros((2 * bq, DIFF_DV), F32))
    carry = lax.fori_loop(0, qi, functools.partial(step, masked=False), init)
    _, l, acc = step(qi, carry, True)
    on = acc / l
    lam = _diff_lambda(lq1_ref, lk1_ref, lq2_ref, lk2_ref, lambda_init)
    o = on[:bq] - lam * on[bq:]
    o_ref[0] = _rms_head(o, nw_ref[...], lambda_init).astype(o_ref.dtype)


def _diffattn(dq, dkb, dvb, lam_params, norm_w, lambda_init, bq):
    b, s, _ = dq.shape
    vec = _const_spec((1, DIFF_DK))
    return pl.pallas_call(
        functools.partial(_diffattn_kernel, bq=bq, lambda_init=lambda_init),
        out_shape=jax.ShapeDtypeStruct((b, s, DIFF_V_W), BF16),
        grid=(b, DIFF_HEADS, s // bq),
        in_specs=[pl.BlockSpec((1, bq, 2 * DIFF_DK), lambda i, h, j: (i, j, h)),
                  pl.BlockSpec((1, s, 2 * DIFF_DK), lambda i, h, j: (i, 0, h)),
                  pl.BlockSpec((1, s, DIFF_DV), lambda i, h, j: (i, 0, h)),
                  vec, vec, vec, vec, _const_spec((1, DIFF_DV))],
        out_specs=pl.BlockSpec((1, bq, DIFF_DV), lambda i, h, j: (i, j, h)),
        compiler_params=_cparams(("parallel", "parallel", "arbitrary")),
        name="diffattn",
    )(dq, dkb, dvb, *lam_params, norm_w)


def _decode_kernel(pt_ref, *refs, n_steps, lambda_init):
    npg = PAGES_PER_STEP
    k_refs = refs[:npg]
    v_refs = refs[npg:2 * npg]
    (q_ref, kn_ref, vn_ref, lq1_ref, lk1_ref, lq2_ref, lk2_ref, nw_ref,
     o_ref, m_ref, l_ref, acc_ref) = refs[2 * npg:]
    j = pl.program_id(1)

    @pl.when(j == 0)
    def _():
        m_ref[...] = jnp.full(m_ref.shape, -jnp.inf, F32)
        l_ref[...] = jnp.zeros(l_ref.shape, F32)
        acc_ref[...] = jnp.zeros(acc_ref.shape, F32)

    q = q_ref[0]

    def update(kb, vb, mask):
        s = _dot_nt(q, kb)
        if mask is not None:
            s = jnp.where(mask, s, -jnp.inf)
        m = m_ref[...]
        m_new = jnp.maximum(m, jnp.max(s, axis=-1, keepdims=True))
        p = jnp.exp(s - m_new)
        alpha = jnp.exp(m - m_new)
        l_ref[...] = alpha * l_ref[...] + jnp.sum(p, axis=-1, keepdims=True)
        acc_ref[...] = alpha * acc_ref[...] + _dot(p.astype(BF16), vb)
        m_ref[...] = m_new

    for i in range(npg):
        update(k_refs[i][0].astype(BF16), v_refs[i][0].astype(BF16), None)

    @pl.when(j == n_steps - 1)
    def _():
        rows = 4 * 8
        r = lax.broadcasted_iota(jnp.int32, (rows, PAGE_SIZE), 0)
        cidx = lax.broadcasted_iota(jnp.int32, (rows, PAGE_SIZE), 1)
        update(kn_ref[0], vn_ref[0], cidx <= (r >> 3))
        lam = _diff_lambda(lq1_ref, lk1_ref, lq2_ref, lk2_ref, lambda_init)
        on = acc_ref[...] / l_ref[...]
        rr = lax.broadcasted_iota(jnp.int32, on.shape, 0)
        cc = lax.broadcasted_iota(jnp.int32, on.shape, 1)
        g = rr & 7
        coef = jnp.where((g & 1) == 0, 1.0, -lam)
        w = jnp.where((cc >> 7) == (g >> 1), coef, 0.0)
        o = jnp.sum((on * w).reshape(4, 8, DIFF_V_W), axis=1)
        for h in range(DIFF_HEADS):
            sl = slice(h * DIFF_DV, (h + 1) * DIFF_DV)
            o_ref[0, :, sl] = _rms_head(o[:, sl], nw_ref[...], lambda_init)


def _decode_attn(page_table, cache_k, cache_v, qbd, k_new, v_new, lam_params, norm_w, lambda_init):
    bd, n_pages = page_table.shape
    npg = PAGES_PER_STEP
    n_steps = n_pages // npg

    def page_spec(i):
        return pl.BlockSpec((1, PAGE_SIZE, DIFF_QK_W), lambda b, j, pt: (pt[b, j * npg + i], 0, 0))

    per_seq = lambda r: pl.BlockSpec((1, r, DIFF_QK_W), lambda b, j, pt: (b, 0, 0))
    vec = pl.BlockSpec((1, DIFF_DK), lambda b, j, pt: (0, 0))
    grid_spec = pltpu.PrefetchScalarGridSpec(
        num_scalar_prefetch=1,
        grid=(bd, n_steps),
        in_specs=[page_spec(i) for i in range(npg)] + [page_spec(i) for i in range(npg)]
        + [per_seq(32), per_seq(PAGE_SIZE), per_seq(PAGE_SIZE), vec, vec, vec, vec,
           pl.BlockSpec((1, DIFF_DV), lambda b, j, pt: (0, 0))],
        out_specs=per_seq(4),
        scratch_shapes=[pltpu.VMEM((32, 1), F32), pltpu.VMEM((32, 1), F32), pltpu.VMEM((32, DIFF_V_W), F32)],
    )
    return pl.pallas_call(
        functools.partial(_decode_kernel, n_steps=n_steps, lambda_init=lambda_init),
        out_shape=jax.ShapeDtypeStruct((bd, 4, DIFF_V_W), F32),
        grid_spec=grid_spec,
        compiler_params=_cparams(("parallel", "arbitrary")),
        name="decode_attn",
    )(page_table, *([cache_k] * npg), *([cache_v] * npg), qbd, k_new, v_new, *lam_params, norm_w)


def _layer_norm(x, w, b):
    mu = jnp.mean(x, axis=-1, keepdims=True)
    xc = x - mu
    var = jnp.mean(xc * xc, axis=-1, keepdims=True)
    return xc * lax.rsqrt(var + LN_EPS) * w + b


def _tail_kernel(x_ref, on_ref, od_ref, wg_ref, rnw_ref, wur_ref, wud_ref, wo_ref, l1w_ref, l1b_ref,
                 sw1_ref, sw3_ref, sw2_ref, h_ref, base_ref, *, alpha):
    x = x_ref[...]
    xb = x.astype(BF16)
    rg = _dot(xb, wg_ref[:, 0:1024])
    o_ret = _silu(rg) * (on_ref[...].astype(F32) * rnw_ref[...])
    u = _sigmoid(_dot(xb, wg_ref[:, 1024:2048])) * _dot(o_ret.astype(BF16), wur_ref[...])
    u = u + _sigmoid(_dot(xb, wg_ref[:, 2048:3072])) * _dot(od_ref[...], wud_ref[...])
    h = _layer_norm(alpha * x + _dot(u.astype(BF16), wo_ref[...]), l1w_ref[...], l1b_ref[...])
    hb = h.astype(BF16)
    a = _silu(_dot(hb, sw1_ref[...])) * _dot(hb, sw3_ref[...])
    h_ref[...] = h
    base_ref[...] = alpha * h + _dot(a.astype(BF16), sw2_ref[...])


def _tail(x2d, on, od, w, alpha, tm):
    t = x2d.shape[0]
    row = lambda wd: pl.BlockSpec((tm, wd), lambda i: (i, 0))
    names = ("wg", "ret_norm_w", "w_up_ret", "w_up_diff", "w_out", "ln1_w", "ln1_b", "sw1", "sw3", "sw2")
    ws = [w[n] for n in names]
    return pl.pallas_call(
        functools.partial(_tail_kernel, alpha=alpha),
        out_shape=(jax.ShapeDtypeStruct((t, D_MODEL), F32), jax.ShapeDtypeStruct((t, D_MODEL), F32)),
        grid=(t // tm,),
        in_specs=[row(D_MODEL), row(RET_V_W), row(DIFF_V_W)] + [_const_spec(a.shape) for a in ws],
        out_specs=(row(D_MODEL), row(D_MODEL)),
        compiler_params=_cparams(("parallel",)),
        name="tail",
    )(x2d, on, od, *ws)


def _first_argmax(x, iota, size):
    m = jnp.max(x, axis=0, keepdims=True)
    idx = jnp.min(jnp.where(x == m, iota, size), axis=0, keepdims=True)
    return m, idx


def _router_kernel(h_ref, rw_ref, rb_ref, eidx_ref, gate_ref, pos_ref, cnt_ref, carry_ref):
    i = pl.program_id(0)
    tb = h_ref.shape[0]
    per = N_EXPERTS // N_GROUPS

    @pl.when(i == 0)
    def _():
        carry_ref[...] = jnp.zeros(carry_ref.shape, F32)

    s = _sigmoid(_dot_nt(rw_ref[...], h_ref[...].astype(BF16)))
    sb = s + rb_ref[...]
    neg = -jnp.inf

    sb3 = sb.reshape(N_GROUPS, per, tb)
    io3 = lax.broadcasted_iota(jnp.int32, sb3.shape, 1)
    m1 = jnp.max(sb3, axis=1, keepdims=True)
    i1 = jnp.min(jnp.where(sb3 == m1, io3, per), axis=1, keepdims=True)
    m2 = jnp.max(jnp.where(io3 == i1, neg, sb3), axis=1, keepdims=True)
    gscore = (m1 + m2).reshape(N_GROUPS, tb)

    iog = lax.broadcasted_iota(jnp.int32, gscore.shape, 0)
    gsel = jnp.zeros(gscore.shape, F32)
    for _ in range(TOPK_GROUPS):
        _, gi = _first_argmax(gscore, iog, N_GROUPS)
        hit = iog == gi
        gsel = jnp.where(hit, 1.0, gsel)
        gscore = jnp.where(hit, neg, gscore)

    emask = jnp.broadcast_to(gsel.reshape(N_GROUPS, 1, tb), (N_GROUPS, per, tb)).reshape(N_EXPERTS, tb)
    cand = jnp.where(emask > 0.0, sb, neg)
    ioe = lax.broadcasted_iota(jnp.int32, cand.shape, 0)
    chosen = jnp.zeros(cand.shape, F32)
    idxs, gates = [], []
    for _ in range(TOP_K):
        _, ei = _first_argmax(cand, ioe, N_EXPERTS)
        hit = ioe == ei
        gates.append(jnp.sum(jnp.where(hit, s, 0.0), axis=0, keepdims=True))
        idxs.append(ei)
        chosen = jnp.where(hit, 1.0, chosen)
        cand = jnp.where(hit, neg, cand)

    gsum = gates[0]
    for g in gates[1:]:
        gsum = gsum + g
    scale = ROUTED_SCALE / gsum
    for k in range(TOP_K):
        eidx_ref[k:k + 1, :] = idxs[k]
        gate_ref[k:k + 1, :] = gates[k] * scale

    r = lax.broadcasted_iota(jnp.int32, (tb, tb), 0)
    cidx = lax.broadcasted_iota(jnp.int32, (tb, tb), 1)
    before = jnp.where(r < cidx, 1.0, 0.0).astype(BF16)
    rank = carry_ref[...] + _dot(chosen.astype(BF16), before)
    for k in range(TOP_K):
        pk = jnp.sum(jnp.where(ioe == idxs[k], rank, 0.0), axis=0, keepdims=True)
        pos_ref[k:k + 1, :] = pk.astype(jnp.int32)
    total = carry_ref[...] + jnp.sum(chosen, axis=1, keepdims=True)
    carry_ref[...] = total
    cnt_ref[...] = total.astype(jnp.int32)


def _router(h, rw_t, rb_col, tb):
    t = h.shape[0]
    slot = pl.BlockSpec((TOP_K, tb), lambda i: (0, i))
    return pl.pallas_call(
        _router_kernel,
        out_shape=(jax.ShapeDtypeStruct((TOP_K, t), jnp.int32), jax.ShapeDtypeStruct((TOP_K, t), F32),
                   jax.ShapeDtypeStruct((TOP_K, t), jnp.int32), jax.ShapeDtypeStruct((N_EXPERTS, 1), jnp.int32)),
        grid=(t // tb,),
        in_specs=[pl.BlockSpec((tb, D_MODEL), lambda i: (i, 0)), _const_spec(rw_t.shape), _const_spec(rb_col.shape)],
        out_specs=(slot, slot, slot, _const_spec((N_EXPERTS, 1))),
        scratch_shapes=[pltpu.VMEM((N_EXPERTS, 1), F32)],
        compiler_params=_cparams(("arbitrary",)),
        name="router",
    )(h, rw_t, rb_col)


def _row_copy_wait(src_rows, dst_rows, sem):
    pltpu.make_async_copy(src_rows, dst_rows, sem).wait()


def _dispatch_kernel(dest_ref, h_ref, xs_in_ref, xs_ref, sem):
    del xs_in_ref
    tm = h_ref.shape[0]

    def body(t, carry):
        for k in range(TOP_K):
            pltpu.make_async_copy(h_ref.at[pl.ds(t, 1), :], xs_ref.at[pl.ds(dest_ref[k, t], 1), :], sem).start()
        return carry

    lax.fori_loop(0, tm, body, 0)
    for k in range(TOP_K):
        _row_copy_wait(h_ref, xs_ref.at[pl.ds(0, tm), :], sem)


def _dispatch(dest, h, n_rows, tm):
    t = h.shape[0]
    xs0 = jnp.zeros((n_rows, D_MODEL), F32)
    return pl.pallas_call(
        _dispatch_kernel,
        out_shape=jax.ShapeDtypeStruct((n_rows, D_MODEL), F32),
        grid=(t // tm,),
        in_specs=[pl.BlockSpec((TOP_K, tm), lambda i: (0, i), memory_space=pltpu.SMEM),
                  pl.BlockSpec((tm, D_MODEL), lambda i: (i, 0)),
                  pl.BlockSpec(memory_space=pl.ANY)],
        out_specs=pl.BlockSpec(memory_space=pl.ANY),
        scratch_shapes=[pltpu.SemaphoreType.DMA(())],
        input_output_aliases={2: 0},
        compiler_params=_cparams(("arbitrary",)),
        name="dispatch",
    )(dest, h, xs0)


def _expert_kernel(be_ref, nb_ref, xs_ref, w1_ref, w3_ref, w2_ref, ys_ref):
    del be_ref
    i = pl.program_id(0)

    @pl.when(i < nb_ref[0])
    def _():
        xb = xs_ref[...].astype(BF16)
        a = _silu(_dot(xb, w1_ref[0].astype(BF16))) * _dot(xb, w3_ref[0].astype(BF16))
        ys_ref[...] = _dot(a.astype(BF16), w2_ref[0].astype(BF16))

    @pl.when(i >= nb_ref[0])
    def _():
        ys_ref[...] = jnp.zeros(ys_ref.shape, F32)


def _experts(block_expert, n_used, xs, w1, w3, w2):
    n_rows = xs.shape[0]
    grid_spec = pltpu.PrefetchScalarGridSpec(
        num_scalar_prefetch=2,
        grid=(n_rows // MOE_BLOCK,),
        in_specs=[pl.BlockSpec((MOE_BLOCK, D_MODEL), lambda i, be, nb: (i, 0)),
                  pl.BlockSpec((1, D_MODEL, D_EXPERT), lambda i, be, nb: (be[i], 0, 0)),
                  pl.BlockSpec((1, D_MODEL, D_EXPERT), lambda i, be, nb: (be[i], 0, 0)),
                  pl.BlockSpec((1, D_EXPERT, D_MODEL), lambda i, be, nb: (be[i], 0, 0))],
        out_specs=pl.BlockSpec((MOE_BLOCK, D_MODEL), lambda i, be, nb: (i, 0)),
    )
    return pl.pallas_call(
        _expert_kernel,
        out_shape=jax.ShapeDtypeStruct((n_rows, D_MODEL), F32),
        grid_spec=grid_spec,
        compiler_params=_cparams(("arbitrary",)),
        name="experts",
    )(block_expert, n_used, xs, w1, w3, w2)


def _combine_kernel(dcur_ref, dnxt_ref, base_ref, gate_ref, p_ref, ys_ref, l2w_ref, l2b_ref, wpg_ref, wpe_ref,
                    y_ref, rows_ref, sems, *, alpha):
    i = pl.program_id(0)
    n = pl.num_programs(0)
    tm = base_ref.shape[0]

    def start_gather(d_ref, slot):
        def body(t, carry):
            for k in range(TOP_K):
                pltpu.make_async_copy(ys_ref.at[pl.ds(d_ref[k, t], 1), :],
                                      rows_ref.at[slot, k, pl.ds(t, 1), :], sems.at[slot]).start()
            return carry

        lax.fori_loop(0, tm, body, 0)

    @pl.when(i == 0)
    def _():
        start_gather(dcur_ref, 0)

    slot = i % 2

    @pl.when(i + 1 < n)
    def _():
        start_gather(dnxt_ref, 1 - slot)

    for k in range(TOP_K):
        _row_copy_wait(ys_ref.at[pl.ds(0, tm), :], rows_ref.at[slot, k], sems.at[slot])

    g = gate_ref[...]
    m = base_ref[...]
    for k in range(TOP_K):
        m = m + rows_ref[slot, k] * g[:, k:k + 1]
    h2 = _layer_norm(m, l2w_ref[...], l2b_ref[...])
    emb = _dot(p_ref[...].astype(BF16), wpe_ref[...])
    y_ref[...] = h2 + _sigmoid(_dot(h2.astype(BF16), wpg_ref[...])) * emb


def _combine(dest, base, gate_t, p2d, ys, w, alpha, tm):
    t = base.shape[0]
    n = t // tm
    row = lambda wd: pl.BlockSpec((tm, wd), lambda i: (i, 0))
    ws = [w[nm] for nm in ("ln2_w", "ln2_b", "w_pg", "w_pe")]
    return pl.pallas_call(
        functools.partial(_combine_kernel, alpha=alpha),
        out_shape=jax.ShapeDtypeStruct((t, D_MODEL), F32),
        grid=(n,),
        in_specs=[pl.BlockSpec((TOP_K, tm), lambda i: (0, i), memory_space=pltpu.SMEM),
                  pl.BlockSpec((TOP_K, tm), lambda i: (0, jnp.minimum(i + 1, n - 1)), memory_space=pltpu.SMEM),
                  row(D_MODEL), row(TOP_K), row(p2d.shape[1]),
                  pl.BlockSpec(memory_space=pl.ANY)] + [_const_spec(a.shape) for a in ws],
        out_specs=row(D_MODEL),
        scratch_shapes=[pltpu.VMEM((2, TOP_K, tm, D_MODEL), F32), pltpu.SemaphoreType.DMA((2,))],
        compiler_params=_cparams(("arbitrary",)),
        name="combine",
    )(dest, dest, base, gate_t, p2d, ys, *ws)


def _moe_and_out(h, base, p2d, w, alpha, tb, tm_disp, tm_comb):
    t = h.shape[0]
    eidx, gate, pos, counts = _router(h, w["router_wt"], w["router_b"], tb)
    counts = counts[:, 0]
    padded = (counts + MOE_BLOCK - 1) // MOE_BLOCK * MOE_BLOCK
    pad_end = jnp.cumsum(padded)
    pad_start = pad_end - padded
    dest = pad_start[eidx] + pos
    n_blocks = (t * TOP_K + N_EXPERTS * (MOE_BLOCK - 1) + MOE_BLOCK - 1) // MOE_BLOCK
    block_expert = jnp.minimum(
        jnp.searchsorted(pad_end, jnp.arange(n_blocks, dtype=jnp.int32) * MOE_BLOCK, side="right"),
        N_EXPERTS - 1).astype(jnp.int32)
    n_used = (pad_end[-1:] // MOE_BLOCK).astype(jnp.int32)
    xs = _dispatch(dest, h, n_blocks * MOE_BLOCK, tm_disp)
    ys = _experts(block_expert, n_used, xs, w["ew1"], w["ew3"], w["ew2"])
    return _combine(dest, base, gate.T, p2d, ys, w, alpha, tm_comb)


def _layer_weights(i, w_in, ret_norm_w, diff_norm_w, w_up_ret, w_up_diff, w_out, ln1_w, ln1_b, router_w, router_b,
                   expert_w1, expert_w3, expert_w2, shared_w1, shared_w3, shared_w2, ln2_w, ln2_b, w_pe, w_pg):
    o = _OFFS
    wi = w_in[i]
    vec = lambda a: a[i].reshape(1, -1).astype(F32)
    return {
        "w_mix": jnp.concatenate([wi[:, o[0]:o[3]], wi[:, o[4]:o[7]]], axis=1).astype(BF16),
        "wg": jnp.concatenate([wi[:, o[3]:o[4]], wi[:, o[7]:o[9]]], axis=1).astype(BF16),
        "ret_norm_w": vec(ret_norm_w), "diff_norm_w": vec(diff_norm_w),
        "w_up_ret": w_up_ret[i].astype(BF16), "w_up_diff": w_up_diff[i].astype(BF16), "w_out": w_out[i].astype(BF16),
        "ln1_w": vec(ln1_w), "ln1_b": vec(ln1_b), "ln2_w": vec(ln2_w), "ln2_b": vec(ln2_b),
        "router_wt": router_w[i].T.astype(BF16), "router_b": router_b[i].reshape(-1, 1).astype(F32),
        "ew1": expert_w1[i], "ew3": expert_w3[i], "ew2": expert_w2[i],
        "sw1": shared_w1[i].astype(BF16), "sw3": shared_w3[i].astype(BF16), "sw2": shared_w2[i].astype(BF16),
        "w_pe": w_pe[i].astype(BF16), "w_pg": w_pg[i].astype(BF16),
    }


def _pick(n, pref):
    return pref if n % pref == 0 else n


def kernel(x_prompt, x_sample, cache_diff_k, cache_diff_v, state_retention, page_table, p_prompt, p_sample, w_in, ret_norm_w, diff_lq1, diff_lk1, diff_lq2, diff_lk2, diff_norm_w, w_up_ret, w_up_diff, w_out, ln1_w, ln1_b, router_w, router_b, expert_w1, expert_w3, expert_w2, shared_w1, shared_w3, shared_w2, ln2_w, ln2_b, w_pe, w_pg):
    depth = w_in.shape[0]
    b, s, d = x_prompt.shape
    bd, ls, _ = x_sample.shape
    n_pages = page_table.shape[1]
    past = n_pages * PAGE_SIZE
    alpha = (2 * depth) ** 0.25
    tp, ts = b * s, bd * ls

    cos_p, sin_p = _rotation_tables(jnp.arange(s))
    cos_s, sin_s = _rotation_tables(past + jnp.arange(ls))
    cos_s = jnp.tile(cos_s, (bd, 1))
    sin_s = jnp.tile(sin_s, (bd, 1))

    yp, ys = x_prompt.reshape(tp, d), x_sample.reshape(ts, d)
    kp_l, vp_l, sp_l, ks_l, vs_l, ss_l = [], [], [], [], [], []
    for i in range(depth):
        lambda_init = 0.8 - 0.6 * math.exp(-0.3 * i)
        w = _layer_weights(i, w_in, ret_norm_w, diff_norm_w, w_up_ret, w_up_diff, w_out, ln1_w, ln1_b, router_w,
                           router_b, expert_w1, expert_w3, expert_w2, shared_w1, shared_w3, shared_w2,
                           ln2_w, ln2_b, w_pe, w_pg)
        lam_params = tuple(a[i].reshape(1, DIFF_DK).astype(F32) for a in (diff_lq1, diff_lk1, diff_lq2, diff_lk2))

        rq, rk, rv, dq, dk, dv, dkb, dvb = _proj(yp, w["w_mix"], cos_p, sin_p, _pick(s, 512))
        seq = lambda a: a.reshape(b, s, a.shape[-1])
        on, st_p = _retention(seq(rq), seq(rk), seq(rv), jnp.zeros((b, RET_HEADS, RET_DK, RET_DV), F32),
                              RET_CHUNK, _pick(s, 512))
        od = _diffattn(seq(dq), seq(dkb), seq(dvb), lam_params, w["diff_norm_w"], lambda_init, _pick(s, 512))
        h, base = _tail(yp, on.reshape(tp, RET_V_W), od.reshape(tp, DIFF_V_W), w, alpha, _pick(tp, 256))
        yp = _moe_and_out(h, base, p_prompt[i].reshape(tp, -1), w, alpha, _pick(tp, 512), _pick(tp, 256),
                          _pick(tp, 128))
        kp_l.append(dk.reshape(b, s, DIFF_HEADS, 2, DIFF_DK))
        vp_l.append(dv.reshape(b, s, DIFF_HEADS, DIFF_DV))
        sp_l.append(st_p)

        rq, rk, rv, dq, dk, dv, dkb, dvb = _proj(ys, w["w_mix"], cos_s, sin_s, _pick(ts, 512))
        pad_rows = lambda a, r: jnp.pad(a.reshape(bd, ls, a.shape[-1]), ((0, 0), (0, r - ls), (0, 0)))
        on, st_s = _retention(pad_rows(rq, RET_CHUNK), pad_rows(rk, RET_CHUNK), pad_rows(rv, RET_CHUNK),
                              state_retention[i], ls, RET_CHUNK)
        on = on[:, :ls].reshape(ts, RET_V_W)
        q5 = dq.reshape(bd, ls, 1, 2 * DIFF_HEADS, DIFF_DK)
        eye = jnp.eye(2 * DIFF_HEADS, dtype=BF16)[None, None, :, :, None]
        qbd = (q5 * eye).reshape(bd, ls * 2 * DIFF_HEADS, DIFF_QK_W)
        od = _decode_attn(page_table, cache_diff_k[i].reshape(-1, PAGE_SIZE, DIFF_QK_W),
                          cache_diff_v[i].reshape(-1, PAGE_SIZE, DIFF_V_W), qbd,
                          pad_rows(dkb, PAGE_SIZE), pad_rows(dvb, PAGE_SIZE), lam_params, w["diff_norm_w"],
                          lambda_init)
        od = od.reshape(ts, DIFF_V_W).astype(BF16)
        h, base = _tail(ys, on, od, w, alpha, _pick(ts, 256))
        ys = _moe_and_out(h, base, p_sample[i].reshape(ts, -1), w, alpha, _pick(ts, 512), _pick(ts, 256),
                          _pick(ts, 128))
        ks_l.append(dk.reshape(bd, ls, DIFF_HEADS, 2, DIFF_DK))
        vs_l.append(dv.reshape(bd, ls, DIFF_HEADS, DIFF_DV))
        ss_l.append(st_s)

    return (yp.reshape(b, s, d), ys.reshape(bd, ls, d), jnp.stack(kp_l), jnp.stack(vp_l), jnp.stack(sp_l),
            jnp.stack(ks_l), jnp.stack(vs_l), jnp.stack(ss_l))
```

```python
import functools
import math

import jax
import jax.numpy as jnp
import numpy as np
from jax import lax
from jax.experimental import pallas as pl
from jax.experimental.pallas import tpu as pltpu

F32 = jnp.float32
BF16 = jnp.bfloat16

D_MODEL = 1024
PAGE_SIZE = 128
RET_HEADS = 4
RET_DK = 128
RET_DV = 256
RET_CHUNK = 128
DIFF_HEADS = 4
DIFF_DK = 64
DIFF_DV = 128
N_EXPERTS = 256
TOP_K = 8
N_GROUPS = 8
TOPK_GROUPS = 4
D_EXPERT = 256
ROUTED_SCALE = 2.5
LN_EPS = 1e-5

RET_QK_W = RET_HEADS * RET_DK
RET_V_W = RET_HEADS * RET_DV
DIFF_QK_W = DIFF_HEADS * 2 * DIFF_DK
DIFF_V_W = DIFF_HEADS * DIFF_DV
_SPLITS = (RET_QK_W, RET_QK_W, RET_V_W, RET_V_W, DIFF_QK_W, DIFF_QK_W, DIFF_V_W, D_MODEL, D_MODEL)
_OFFS = tuple(int(o) for o in np.cumsum((0,) + _SPLITS))

LANES = 128
VMEM_LIMIT = 56 * 1024 * 1024

MOE_BLOCK = 256
PAGES_PER_STEP = 8


def _cparams(sem, vmem=VMEM_LIMIT):
    return pltpu.CompilerParams(dimension_semantics=sem, vmem_limit_bytes=vmem)


def _const_spec(shape):
    nd = len(shape)
    return pl.BlockSpec(shape, lambda *_: (0,) * nd)


def _sigmoid(x):
    return 1.0 / (1.0 + jnp.exp(-x))


def _silu(x):
    return x * _sigmoid(x)


PACK_W = D_MODEL // 2
_HI_MASK = np.uint32(0xFFFF0000)


def _pack_row(x):
    bits = lambda v: lax.bitcast_convert_type(v.astype(BF16).astype(F32), jnp.uint32)
    return (bits(x[:, :PACK_W]) >> 16) | (bits(x[:, PACK_W:]) & _HI_MASK)


def _unpack_row(p):
    lo = lax.bitcast_convert_type(p << 16, F32)
    hi = lax.bitcast_convert_type(p & _HI_MASK, F32)
    return jnp.concatenate([lo, hi], axis=1)


def _dot(a, b):
    return jnp.dot(a, b, preferred_element_type=F32)


def _dot_nt(a, b):
    return lax.dot_general(a, b, (((1,), (1,)), ((), ())), preferred_element_type=F32)


def _dot_tn(a, b):
    return lax.dot_general(a, b, (((0,), (0,)), ((), ())), preferred_element_type=F32)


def _swap_pairs(x):
    lane = lax.broadcasted_iota(jnp.int32, x.shape, 1)
    nxt = pltpu.roll(x, LANES - 1, 1)
    prv = pltpu.roll(x, 1, 1)
    return jnp.where((lane & 1) == 0, nxt, prv)


def _proj_kernel(x_ref, w_ref, cos_ref, sin_ref,
                 rq_ref, rk_ref, rv_ref, dq_ref, dk_ref, dv_ref, dkb_ref, dvb_ref):
    xb = x_ref[...].astype(BF16)
    cos = cos_ref[...]
    sin = sin_ref[...]

    def mm(lo, hi):
        return _dot(xb, w_ref[:, lo:hi])

    q = mm(0, 512)
    k = mm(512, 1024)
    for h in range(RET_HEADS):
        sl = slice(h * RET_DK, (h + 1) * RET_DK)
        qh = q[:, sl]
        kh = k[:, sl]
        rq_ref[:, sl] = (qh * cos + _swap_pairs(qh) * sin).astype(BF16)
        rk_ref[:, sl] = ((kh * cos + _swap_pairs(kh) * sin) * (RET_DK ** -0.5)).astype(BF16)
    rv_ref[...] = mm(1024, 2048).astype(BF16)
    dq_ref[...] = (mm(2048, 2560) * (DIFF_DK ** -0.5)).astype(BF16)
    dk = mm(2560, 3072)
    dk_ref[...] = dk
    dkb_ref[...] = dk.astype(BF16)
    dv = mm(3072, 3584)
    dv_ref[...] = dv
    dvb_ref[...] = dv.astype(BF16)


def _proj(x2d, w_mix, cos_t, sin_t, tm):
    t = x2d.shape[0]
    nt = cos_t.shape[0] // tm
    row = lambda w: pl.BlockSpec((tm, w), lambda i: (i, 0))
    tab = pl.BlockSpec((tm, LANES), lambda i: (i % nt, 0))
    out_shapes = (
        jax.ShapeDtypeStruct((t, RET_QK_W), BF16), jax.ShapeDtypeStruct((t, RET_QK_W), BF16),
        jax.ShapeDtypeStruct((t, RET_V_W), BF16), jax.ShapeDtypeStruct((t, DIFF_QK_W), BF16),
        jax.ShapeDtypeStruct((t, DIFF_QK_W), F32), jax.ShapeDtypeStruct((t, DIFF_V_W), F32),
        jax.ShapeDtypeStruct((t, DIFF_QK_W), BF16), jax.ShapeDtypeStruct((t, DIFF_V_W), BF16),
    )
    return pl.pallas_call(
        _proj_kernel,
        out_shape=out_shapes,
        grid=(t // tm,),
        in_specs=[row(D_MODEL), _const_spec(w_mix.shape), tab, tab],
        out_specs=(row(512), row(512), row(1024), row(512), row(512), row(512), row(512), row(512)),
        compiler_params=_cparams(("parallel",)),
        name="proj",
    )(x2d, w_mix, cos_t, sin_t)


def _rotation_tables(pos):
    inv = 1.0 / (10000.0 ** jnp.linspace(0.0, 1.0, RET_DK // 2))
    ang = pos.astype(F32)[:, None] * inv[None, :]
    cos = jnp.repeat(jnp.cos(ang), 2, axis=1)
    sin = jnp.sin(ang)
    sin = jnp.stack([-sin, sin], axis=-1).reshape(pos.shape[0], RET_DK)
    return cos, sin


def _retention_kernel(q_ref, k_ref, v_ref, s0_ref, dm_ref, qd_ref, kd_ref, o_ref, s_ref, st_ref,
                      *, n_sub, gl):
    c = pl.program_id(1)

    @pl.when(c == 0)
    def _():
        st_ref[...] = s0_ref[0]

    for j in range(n_sub):
        rows = slice(j * RET_CHUNK, (j + 1) * RET_CHUNK)
        for h in range(RET_HEADS):
            q = q_ref[0, rows, h * RET_DK:(h + 1) * RET_DK]
            k = k_ref[0, rows, h * RET_DK:(h + 1) * RET_DK]
            v = v_ref[0, rows, h * RET_DV:(h + 1) * RET_DV]
            st = st_ref[h]
            qk = _dot_nt(q, k) * dm_ref[h]
            q_dec = (q.astype(F32) * qd_ref[h]).astype(BF16)
            o = _dot(qk.astype(BF16), v) + _dot(q_dec, st.astype(BF16))
            k_dec = (k.astype(F32) * kd_ref[h]).astype(BF16)
            st_ref[h] = gl[h] * st + _dot_tn(k_dec, v)
            mu = jnp.mean(o, axis=-1, keepdims=True)
            oc = o - mu
            var = jnp.mean(oc * oc, axis=-1, keepdims=True)
            o_ref[0, rows, h * RET_DV:(h + 1) * RET_DV] = (oc * lax.rsqrt(var + LN_EPS)).astype(BF16)

    @pl.when(c == pl.num_programs(1) - 1)
    def _():
        s_ref[0] = st_ref[...]


def _retention_tables(length):
    lg = np.log1p(-(2.0 ** (-5.0 - np.arange(RET_HEADS, dtype=np.float64))))
    idx = np.arange(RET_CHUNK, dtype=np.float64)
    rel = idx[:, None] - idx[None, :]
    valid = (idx < length)
    dm = np.where((rel >= 0) & valid[:, None] & valid[None, :],
                  np.exp(np.maximum(rel, 0.0)[None] * lg[:, None, None]), 0.0)
    qd = np.exp((idx + 1.0)[None, :] * lg[:, None])
    kd = np.where(valid[None, :], np.exp((length - 1.0 - idx)[None, :] * lg[:, None]), 0.0)
    bc = lambda a: np.ascontiguousarray(np.broadcast_to(a[:, :, None], (RET_HEADS, RET_CHUNK, RET_DK)))
    gl = tuple(float(np.exp(length * g)) for g in lg)
    return (jnp.asarray(dm, F32), jnp.asarray(bc(qd), F32), jnp.asarray(bc(kd), F32)), gl


def _retention(rq, rk, rv, state0, length, lb):
    b, s, _ = rq.shape
    (dm, qd, kd), gl = _retention_tables(length)
    seq = lambda w: pl.BlockSpec((1, lb, w), lambda i, c: (i, c, 0))
    st_spec = pl.BlockSpec((1, RET_HEADS, RET_DK, RET_DV), lambda i, c: (i, 0, 0, 0))
    return pl.pallas_call(
        functools.partial(_retention_kernel, n_sub=lb // RET_CHUNK, gl=gl),
        out_shape=(jax.ShapeDtypeStruct((b, s, RET_V_W), BF16),
                   jax.ShapeDtypeStruct((b, RET_HEADS, RET_DK, RET_DV), F32)),
        grid=(b, s // lb),
        in_specs=[seq(RET_QK_W), seq(RET_QK_W), seq(RET_V_W), st_spec,
                  _const_spec(dm.shape), _const_spec(qd.shape), _const_spec(kd.shape)],
        out_specs=(seq(RET_V_W), st_spec),
        scratch_shapes=[pltpu.VMEM((RET_HEADS, RET_DK, RET_DV), F32)],
        compiler_params=_cparams(("parallel", "arbitrary")),
        name="retention",
    )(rq, rk, rv, state0, dm, qd, kd)


def _diff_lambda(lq1_ref, lk1_ref, lq2_ref, lk2_ref, lambda_init):
    a = jnp.sum(lq1_ref[...] * lk1_ref[...], axis=-1, keepdims=True)
    b = jnp.sum(lq2_ref[...] * lk2_ref[...], axis=-1, keepdims=True)
    return jnp.exp(a) - jnp.exp(b) + lambda_init


def _rms_head(o, nw, lambda_init):
    ms = jnp.mean(o * o, axis=-1, keepdims=True)
    return o * lax.rsqrt(ms + LN_EPS) * nw * (1.0 - lambda_init)


def _diffattn_kernel(q_ref, k_ref, v_ref, lq1_ref, lk1_ref, lq2_ref, lk2_ref, nw_ref, o_ref,
                     *, bq, lambda_init):
    qi = pl.program_id(2)
    q = q_ref[0]
    lane = lax.broadcasted_iota(jnp.int32, q.shape, 1)
    zero = jnp.zeros_like(q)
    qq = jnp.concatenate([jnp.where(lane < DIFF_DK, q, zero), jnp.where(lane >= DIFF_DK, q, zero)], axis=0)

    def step(j, carry, masked):
        m, l, acc = carry
        off = pl.multiple_of(j * bq, bq)
        kb = k_ref[0, pl.ds(off, bq), :]
        vb = v_ref[0, pl.ds(off, bq), :]
        s = _dot_nt(qq, kb)
        if masked:
            r = lax.broadcasted_iota(jnp.int32, s.shape, 0)
            r = jnp.where(r >= bq, r - bq, r)
            cidx = lax.broadcasted_iota(jnp.int32, s.shape, 1)
            s = jnp.where(cidx <= r, s, -jnp.inf)
        m_new = jnp.maximum(m, jnp.max(s, axis=-1, keepdims=True))
        p = jnp.exp(s - m_new)
        alpha = jnp.exp(m - m_new)
        l = alpha * l + jnp.sum(p, axis=-1, keepdims=True)
        acc = alpha * acc + _dot(p.astype(BF16), vb)
        return m_new, l, acc

    init = (jnp.full((2 * bq, 1), -jnp.inf, F32), jnp.zeros((2 * bq, 1), F32),
            jnp.zeros((2 * bq, DIFF_DV), F32))
    carry = lax.fori_loop(0, qi, functools.partial(step, masked=False), init)
    _, l, acc = step(qi, carry, True)
    on = acc / l
    lam = _diff_lambda(lq1_ref, lk1_ref, lq2_ref, lk2_ref, lambda_init)
    o = on[:bq] - lam * on[bq:]
    o_ref[0] = _rms_head(o, nw_ref[...], lambda_init).astype(o_ref.dtype)


def _diffattn(dq, dkb, dvb, lam_params, norm_w, lambda_init, bq):
    b, s, _ = dq.shape
    vec = _const_spec((1, DIFF_DK))
    return pl.pallas_call(
        functools.partial(_diffattn_kernel, bq=bq, lambda_init=lambda_init),
        out_shape=jax.ShapeDtypeStruct((b, s, DIFF_V_W), BF16),
        grid=(b, DIFF_HEADS, s // bq),
        in_specs=[pl.BlockSpec((1, bq, 2 * DIFF_DK), lambda i, h, j: (i, j, h)),
                  pl.BlockSpec((1, s, 2 * DIFF_DK), lambda i, h, j: (i, 0, h)),
                  pl.BlockSpec((1, s, DIFF_DV), lambda i, h, j: (i, 0, h)),
                  vec, vec, vec, vec, _const_spec((1, DIFF_DV))],
        out_specs=pl.BlockSpec((1, bq, DIFF_DV), lambda i, h, j: (i, j, h)),
        compiler_params=_cparams(("parallel", "parallel", "arbitrary")),
        name="diffattn",
    )(dq, dkb, dvb, *lam_params, norm_w)


def _decode_kernel(pt_ref, *refs, n_steps, lambda_init):
    npg = PAGES_PER_STEP
    k_refs = refs[:npg]
    v_refs = refs[npg:2 * npg]
    (q_ref, kn_ref, vn_ref, lq1_ref, lk1_ref, lq2_ref, lk2_ref, nw_ref,
     o_ref, m_ref, l_ref, acc_ref) = refs[2 * npg:]
    j = pl.program_id(1)

    @pl.when(j == 0)
    def _():
        m_ref[...] = jnp.full(m_ref.shape, -jnp.inf, F32)
        l_ref[...] = jnp.zeros(l_ref.shape, F32)
        acc_ref[...] = jnp.zeros(acc_ref.shape, F32)

    q = q_ref[0]

    def update(kt, pv, mask):
        s = _dot(q, kt)
        if mask is not None:
            s = jnp.where(mask, s, -jnp.inf)
        m = m_ref[...]
        m_new = jnp.maximum(m, jnp.max(s, axis=-1, keepdims=True))
        p = jnp.exp(s - m_new)
        alpha = jnp.exp(m - m_new)
        l_ref[...] = alpha * l_ref[...] + jnp.sum(p, axis=-1, keepdims=True)
        acc_ref[...] = alpha * acc_ref[...] + pv(p.astype(BF16))
        m_ref[...] = m_new

    for i in range(npg):
        v_ref = v_refs[i]

        def page_pv(p, v_ref=v_ref):
            return jnp.concatenate(
                [_dot(p, v_ref[0, pl.ds(h, PAGE_SIZE, stride=DIFF_HEADS), :].astype(BF16))
                 for h in range(DIFF_HEADS)], axis=1)

        update(k_refs[i][0].astype(BF16), page_pv, None)

    @pl.when(j == n_steps - 1)
    def _():
        rows = 4 * 8
        r = lax.broadcasted_iota(jnp.int32, (rows, PAGE_SIZE), 0)
        cidx = lax.broadcasted_iota(jnp.int32, (rows, PAGE_SIZE), 1)
        update(kn_ref[0], lambda p: _dot(p, vn_ref[0]), cidx <= (r >> 3))
        lam = _diff_lambda(lq1_ref, lk1_ref, lq2_ref, lk2_ref, lambda_init)
        on = acc_ref[...] / l_ref[...]
        rr = lax.broadcasted_iota(jnp.int32, on.shape, 0)
        cc = lax.broadcasted_iota(jnp.int32, on.shape, 1)
        g = rr & 7
        coef = jnp.where((g & 1) == 0, 1.0, -lam)
        w = jnp.where((cc >> 7) == (g >> 1), coef, 0.0)
        o = jnp.sum((on * w).reshape(4, 8, DIFF_V_W), axis=1)
        for h in range(DIFF_HEADS):
            sl = slice(h * DIFF_DV, (h + 1) * DIFF_DV)
            o_ref[0, :, sl] = _rms_head(o[:, sl], nw_ref[...], lambda_init)


def _decode_attn(page_table, cache_kt, cache_v, qbd, kt_new, v_new, lam_params, norm_w, lambda_init):
    bd, n_pages = page_table.shape
    npg = PAGES_PER_STEP
    n_steps = n_pages // npg

    def page_spec(i):
        return pl.BlockSpec((1, DIFF_QK_W, PAGE_SIZE), lambda b, j, pt: (pt[b, j * npg + i], 0, 0))

    per_seq = lambda r, c: pl.BlockSpec((1, r, c), lambda b, j, pt: (b, 0, 0))
    vec = pl.BlockSpec((1, DIFF_DK), lambda b, j, pt: (0, 0))
    grid_spec = pltpu.PrefetchScalarGridSpec(
        num_scalar_prefetch=1,
        grid=(bd, n_steps),
        in_specs=[page_spec(i) for i in range(npg)] + [page_spec(i) for i in range(npg)]
        + [per_seq(32, DIFF_QK_W), per_seq(DIFF_QK_W, PAGE_SIZE), per_seq(PAGE_SIZE, DIFF_V_W), vec, vec, vec, vec,
           pl.BlockSpec((1, DIFF_DV), lambda b, j, pt: (0, 0))],
        out_specs=per_seq(4, DIFF_V_W),
        scratch_shapes=[pltpu.VMEM((32, 1), F32), pltpu.VMEM((32, 1), F32), pltpu.VMEM((32, DIFF_V_W), F32)],
    )
    return pl.pallas_call(
        functools.partial(_decode_kernel, n_steps=n_steps, lambda_init=lambda_init),
        out_shape=jax.ShapeDtypeStruct((bd, 4, DIFF_V_W), F32),
        grid_spec=grid_spec,
        compiler_params=_cparams(("parallel", "arbitrary")),
        name="decode_attn",
    )(page_table, *([cache_kt] * npg), *([cache_v] * npg), qbd, kt_new, v_new, *lam_params, norm_w)


def _layer_norm(x, w, b):
    mu = jnp.mean(x, axis=-1, keepdims=True)
    xc = x - mu
    var = jnp.mean(xc * xc, axis=-1, keepdims=True)
    return xc * lax.rsqrt(var + LN_EPS) * w + b


def _tail_kernel(x_ref, on_ref, od_ref, wg_ref, rnw_ref, wur_ref, wud_ref, wo_ref, l1w_ref, l1b_ref,
                 sw1_ref, sw3_ref, sw2_ref, h_ref, base_ref, *, alpha):
    x = x_ref[...]
    xb = x.astype(BF16)
    rg = _dot(xb, wg_ref[:, 0:1024])
    o_ret = _silu(rg) * (on_ref[...].astype(F32) * rnw_ref[...])
    u = _sigmoid(_dot(xb, wg_ref[:, 1024:2048])) * _dot(o_ret.astype(BF16), wur_ref[...])
    u = u + _sigmoid(_dot(xb, wg_ref[:, 2048:3072])) * _dot(od_ref[...], wud_ref[...])
    h = _layer_norm(alpha * x + _dot(u.astype(BF16), wo_ref[...]), l1w_ref[...], l1b_ref[...])
    hb = h.astype(BF16)
    a = _silu(_dot(hb, sw1_ref[...])) * _dot(hb, sw3_ref[...])
    h_ref[...] = _pack_row(h)
    base_ref[...] = alpha * h + _dot(a.astype(BF16), sw2_ref[...])


def _tail(x2d, on, od, w, alpha, tm):
    t = x2d.shape[0]
    row = lambda wd: pl.BlockSpec((tm, wd), lambda i: (i, 0))
    names = ("wg", "ret_norm_w", "w_up_ret", "w_up_diff", "w_out", "ln1_w", "ln1_b", "sw1", "sw3", "sw2")
    ws = [w[n] for n in names]
    return pl.pallas_call(
        functools.partial(_tail_kernel, alpha=alpha),
        out_shape=(jax.ShapeDtypeStruct((t, PACK_W), jnp.uint32), jax.ShapeDtypeStruct((t, D_MODEL), F32)),
        grid=(t // tm,),
        in_specs=[row(D_MODEL), row(RET_V_W), row(DIFF_V_W)] + [_const_spec(a.shape) for a in ws],
        out_specs=(row(PACK_W), row(D_MODEL)),
        compiler_params=_cparams(("parallel",)),
        name="tail",
    )(x2d, on, od, *ws)


def _first_argmax(x, iota, size):
    m = jnp.max(x, axis=0, keepdims=True)
    idx = jnp.min(jnp.where(x == m, iota, size), axis=0, keepdims=True)
    return m, idx


def _router_kernel(h_ref, rw_ref, rb_ref, eidx_ref, gate_ref, pos_ref, cnt_ref, carry_ref):
    i = pl.program_id(0)
    tb = h_ref.shape[0]
    per = N_EXPERTS // N_GROUPS

    @pl.when(i == 0)
    def _():
        carry_ref[...] = jnp.zeros(carry_ref.shape, F32)

    s = _sigmoid(_dot_nt(rw_ref[...], _unpack_row(h_ref[...]).astype(BF16)))
    sb = s + rb_ref[...]
    neg = -jnp.inf

    sb3 = sb.reshape(N_GROUPS, per, tb)
    io3 = lax.broadcasted_iota(jnp.int32, sb3.shape, 1)
    m1 = jnp.max(sb3, axis=1, keepdims=True)
    i1 = jnp.min(jnp.where(sb3 == m1, io3, per), axis=1, keepdims=True)
    m2 = jnp.max(jnp.where(io3 == i1, neg, sb3), axis=1, keepdims=True)
    gscore = (m1 + m2).reshape(N_GROUPS, tb)

    iog = lax.broadcasted_iota(jnp.int32, gscore.shape, 0)
    gsel = jnp.zeros(gscore.shape, F32)
    for _ in range(TOPK_GROUPS):
        _, gi = _first_argmax(gscore, iog, N_GROUPS)
        hit = iog == gi
        gsel = jnp.where(hit, 1.0, gsel)
        gscore = jnp.where(hit, neg, gscore)

    emask = jnp.broadcast_to(gsel.reshape(N_GROUPS, 1, tb), (N_GROUPS, per, tb)).reshape(N_EXPERTS, tb)
    cand = jnp.where(emask > 0.0, sb, neg)
    ioe = lax.broadcasted_iota(jnp.int32, cand.shape, 0)
    chosen = jnp.zeros(cand.shape, F32)
    idxs, gates = [], []
    for _ in range(TOP_K):
        _, ei = _first_argmax(cand, ioe, N_EXPERTS)
        hit = ioe == ei
        gates.append(jnp.sum(jnp.where(hit, s, 0.0), axis=0, keepdims=True))
        idxs.append(ei)
        chosen = jnp.where(hit, 1.0, chosen)
        cand = jnp.where(hit, neg, cand)

    gsum = gates[0]
    for g in gates[1:]:
        gsum = gsum + g
    scale = ROUTED_SCALE / gsum
    for k in range(TOP_K):
        eidx_ref[k:k + 1, :] = idxs[k]
        gate_ref[k:k + 1, :] = gates[k] * scale

    r = lax.broadcasted_iota(jnp.int32, (tb, tb), 0)
    cidx = lax.broadcasted_iota(jnp.int32, (tb, tb), 1)
    before = jnp.where(r < cidx, 1.0, 0.0).astype(BF16)
    rank = carry_ref[...] + _dot(chosen.astype(BF16), before)
    for k in range(TOP_K):
        pk = jnp.sum(jnp.where(ioe == idxs[k], rank, 0.0), axis=0, keepdims=True)
        pos_ref[k:k + 1, :] = pk.astype(jnp.int32)
    total = carry_ref[...] + jnp.sum(chosen, axis=1, keepdims=True)
    carry_ref[...] = total
    cnt_ref[...] = total.astype(jnp.int32)


def _router(h, rw_t, rb_col, tb):
    t = h.shape[0]
    slot = pl.BlockSpec((TOP_K, tb), lambda i: (0, i))
    return pl.pallas_call(
        _router_kernel,
        out_shape=(jax.ShapeDtypeStruct((TOP_K, t), jnp.int32), jax.ShapeDtypeStruct((TOP_K, t), F32),
                   jax.ShapeDtypeStruct((TOP_K, t), jnp.int32), jax.ShapeDtypeStruct((N_EXPERTS, 1), jnp.int32)),
        grid=(t // tb,),
        in_specs=[pl.BlockSpec((tb, PACK_W), lambda i: (i, 0)), _const_spec(rw_t.shape), _const_spec(rb_col.shape)],
        out_specs=(slot, slot, slot, _const_spec((N_EXPERTS, 1))),
        scratch_shapes=[pltpu.VMEM((N_EXPERTS, 1), F32)],
        compiler_params=_cparams(("arbitrary",)),
        name="router",
    )(h, rw_t, rb_col)


def _row_copy_wait(src_rows, dst_rows, sem):
    pltpu.make_async_copy(src_rows, dst_rows, sem).wait()


SUBLANES = 8
_PAD_CHUNKS = tuple(1 << s for s in reversed(range(3, MOE_BLOCK.bit_length() - 1)))


def _zero_fill_padding(ps_ref, cnt_ref, zero_ref, xs_ref, sem):
    def chunks(e, act):
        cnt = cnt_ref[e]
        n_pad = (MOE_BLOCK - (cnt & (MOE_BLOCK - 1))) & (MOE_BLOCK - 1)
        off = ps_ref[e] + cnt
        n_single = n_pad & (SUBLANES - 1)
        for j in range(SUBLANES - 1):
            @pl.when(j < n_single)
            def _(j=j):
                act(pltpu.make_async_copy(zero_ref.at[pl.ds(0, 1), :], xs_ref.at[pl.ds(off + j, 1), :], sem))

        off = off + n_single
        for rows in _PAD_CHUNKS:
            take = n_pad & rows

            @pl.when(take != 0)
            def _(off=off, rows=rows):
                dst = xs_ref.at[pl.ds(pl.multiple_of(off, SUBLANES), rows), :]
                act(pltpu.make_async_copy(zero_ref.at[pl.ds(0, rows), :], dst, sem))

            off = off + take

    def start(e, carry):
        chunks(e, lambda cp: cp.start())
        return carry

    def wait(e, carry):
        chunks(e, lambda cp: cp.wait())
        return carry

    lax.fori_loop(0, N_EXPERTS, start, 0)
    lax.fori_loop(0, N_EXPERTS, wait, 0)


def _zero_fill_unused_blocks(nb_ref, zero_ref, xs_ref, sem):
    rows = zero_ref.shape[0]
    n_total = xs_ref.shape[0] // MOE_BLOCK

    def copies(blk, act):
        for j in range(MOE_BLOCK // rows):
            off = pl.multiple_of(blk * MOE_BLOCK + j * rows, rows)
            act(pltpu.make_async_copy(zero_ref, xs_ref.at[pl.ds(off, rows), :], sem))

    def start(blk, carry):
        copies(blk, lambda cp: cp.start())
        return carry

    def wait(blk, carry):
        copies(blk, lambda cp: cp.wait())
        return carry

    lax.fori_loop(nb_ref[0], n_total, start, 0)
    lax.fori_loop(nb_ref[0], n_total, wait, 0)


def _dispatch_kernel(ps_ref, cnt_ref, nb_ref, eidx_ref, pos_ref, h_ref, xs_ref, zero_ref, sem, zsem):
    tm = h_ref.shape[0]

    @pl.when(pl.program_id(0) == 0)
    def _():
        zero_ref[...] = jnp.zeros(zero_ref.shape, zero_ref.dtype)
        _zero_fill_padding(ps_ref, cnt_ref, zero_ref, xs_ref, zsem)
        _zero_fill_unused_blocks(nb_ref, zero_ref, xs_ref, zsem)

    def body(t, carry):
        for k in range(TOP_K):
            dest = ps_ref[eidx_ref[k, t]] + pos_ref[k, t]
            pltpu.make_async_copy(h_ref.at[pl.ds(t, 1), :], xs_ref.at[pl.ds(dest, 1), :], sem).start()
        return carry

    lax.fori_loop(0, tm, body, 0)
    for k in range(TOP_K):
        _row_copy_wait(h_ref, xs_ref.at[pl.ds(0, tm), :], sem)


def _dispatch(pad_start, counts, n_used, eidx, pos, h, n_rows, tm):
    t = h.shape[0]
    slot = pl.BlockSpec((TOP_K, tm), lambda i, ps, cnt, nb: (0, i), memory_space=pltpu.SMEM)
    grid_spec = pltpu.PrefetchScalarGridSpec(
        num_scalar_prefetch=3,
        grid=(t // tm,),
        in_specs=[slot, slot, pl.BlockSpec((tm, PACK_W), lambda i, ps, cnt, nb: (i, 0))],
        out_specs=pl.BlockSpec(memory_space=pl.ANY),
        scratch_shapes=[pltpu.VMEM((_PAD_CHUNKS[0], PACK_W), jnp.uint32), pltpu.SemaphoreType.DMA(()),
                        pltpu.SemaphoreType.DMA(())],
    )
    return pl.pallas_call(
        _dispatch_kernel,
        out_shape=jax.ShapeDtypeStruct((n_rows, PACK_W), jnp.uint32),
        grid_spec=grid_spec,
        compiler_params=_cparams(("arbitrary",)),
        name="dispatch",
    )(pad_start, counts, n_used, eidx, pos, h)


def _expert_kernel(be_ref, nb_ref, xs_ref, w1_ref, w3_ref, w2_ref, ys_ref):
    del be_ref
    i = pl.program_id(0)

    @pl.when(i < nb_ref[0])
    def _():
        xb = _unpack_row(xs_ref[...]).astype(BF16)
        a = _silu(_dot(xb, w1_ref[0].astype(BF16))) * _dot(xb, w3_ref[0].astype(BF16))
        ys_ref[...] = _pack_row(_dot(a.astype(BF16), w2_ref[0].astype(BF16)))

    @pl.when(i >= nb_ref[0])
    def _():
        ys_ref[...] = jnp.zeros(ys_ref.shape, ys_ref.dtype)


def _experts(block_expert, n_used, xs, w1, w3, w2):
    n_rows = xs.shape[0]
    grid_spec = pltpu.PrefetchScalarGridSpec(
        num_scalar_prefetch=2,
        grid=(n_rows // MOE_BLOCK,),
        in_specs=[pl.BlockSpec((MOE_BLOCK, PACK_W), lambda i, be, nb: (jnp.minimum(i, nb[0] - 1), 0)),
                  pl.BlockSpec((1, D_MODEL, D_EXPERT), lambda i, be, nb: (be[i], 0, 0)),
                  pl.BlockSpec((1, D_MODEL, D_EXPERT), lambda i, be, nb: (be[i], 0, 0)),
                  pl.BlockSpec((1, D_EXPERT, D_MODEL), lambda i, be, nb: (be[i], 0, 0))],
        out_specs=pl.BlockSpec((MOE_BLOCK, PACK_W), lambda i, be, nb: (i, 0)),
    )
    return pl.pallas_call(
        _expert_kernel,
        out_shape=jax.ShapeDtypeStruct((n_rows, PACK_W), jnp.uint32),
        grid_spec=grid_spec,
        compiler_params=_cparams(("arbitrary",)),
        name="experts",
    )(block_expert, n_used, xs, w1, w3, w2)


def _combine_kernel(ps_ref, ecur_ref, pcur_ref, enxt_ref, pnxt_ref, base_ref, gate_ref, p_ref, ys_ref,
                    l2w_ref, l2b_ref, wpg_ref, wpe_ref, y_ref, rows_ref, sems):
    i = pl.program_id(0)
    n = pl.num_programs(0)
    tm = base_ref.shape[0]

    def start_gather(e_ref, q_ref, slot):
        def body(t, carry):
            for k in range(TOP_K):
                src = ps_ref[e_ref[k, t]] + q_ref[k, t]
                pltpu.make_async_copy(ys_ref.at[pl.ds(src, 1), :],
                                      rows_ref.at[slot, k, pl.ds(t, 1), :], sems.at[slot]).start()
            return carry

        lax.fori_loop(0, tm, body, 0)

    @pl.when(i == 0)
    def _():
        start_gather(ecur_ref, pcur_ref, 0)

    slot = i % 2

    @pl.when(i + 1 < n)
    def _():
        start_gather(enxt_ref, pnxt_ref, 1 - slot)

    for k in range(TOP_K):
        _row_copy_wait(ys_ref.at[pl.ds(0, tm), :], rows_ref.at[slot, k], sems.at[slot])

    g = gate_ref[...]
    m = base_ref[...]
    for k in range(TOP_K):
        m = m + _unpack_row(rows_ref[slot, k]) * g[:, k:k + 1]
    h2 = _layer_norm(m, l2w_ref[...], l2b_ref[...])
    emb = _dot(p_ref[...].astype(BF16), wpe_ref[...])
    y_ref[...] = h2 + _sigmoid(_dot(h2.astype(BF16), wpg_ref[...])) * emb


def _combine(pad_start, eidx, pos, base, gate_t, p2d, ys, w, tm):
    t = base.shape[0]
    n = t // tm
    row = lambda wd: pl.BlockSpec((tm, wd), lambda i, ps: (i, 0))
    cur = pl.BlockSpec((TOP_K, tm), lambda i, ps: (0, i), memory_space=pltpu.SMEM)
    nxt = pl.BlockSpec((TOP_K, tm), lambda i, ps: (0, jnp.minimum(i + 1, n - 1)), memory_space=pltpu.SMEM)
    ws = [w[nm] for nm in ("ln2_w", "ln2_b", "w_pg", "w_pe")]
    grid_spec = pltpu.PrefetchScalarGridSpec(
        num_scalar_prefetch=1,
        grid=(n,),
        in_specs=[cur, cur, nxt, nxt, row(D_MODEL), row(TOP_K), row(p2d.shape[1]),
                  pl.BlockSpec(memory_space=pl.ANY)]
        + [pl.BlockSpec(a.shape, lambda i, ps, nd=a.ndim: (0,) * nd) for a in ws],
        out_specs=row(D_MODEL),
        scratch_shapes=[pltpu.VMEM((2, TOP_K, tm, PACK_W), jnp.uint32), pltpu.SemaphoreType.DMA((2,))],
    )
    return pl.pallas_call(
        _combine_kernel,
        out_shape=jax.ShapeDtypeStruct((t, D_MODEL), F32),
        grid_spec=grid_spec,
        compiler_params=_cparams(("arbitrary",)),
        name="combine",
    )(pad_start, eidx, pos, eidx, pos, base, gate_t, p2d, ys, *ws)


def _moe_and_out(h, base, p2d, w, tb, tm_disp, tm_comb):
    t = h.shape[0]
    eidx, gate, pos, counts = _router(h, w["router_wt"], w["router_b"], tb)
    counts = counts[:, 0]
    padded = (counts + MOE_BLOCK - 1) // MOE_BLOCK * MOE_BLOCK
    pad_end = jnp.cumsum(padded)
    pad_start = pad_end - padded
    n_blocks = (t * TOP_K + N_EXPERTS * (MOE_BLOCK - 1) + MOE_BLOCK - 1) // MOE_BLOCK
    block_expert = jnp.minimum(
        jnp.searchsorted(pad_end, jnp.arange(n_blocks, dtype=jnp.int32) * MOE_BLOCK, side="right"),
        N_EXPERTS - 1).astype(jnp.int32)
    n_used = (pad_end[-1:] // MOE_BLOCK).astype(jnp.int32)
    xs = _dispatch(pad_start, counts, n_used, eidx, pos, h, n_blocks * MOE_BLOCK, tm_disp)
    ys = _experts(block_expert, n_used, xs, w["ew1"], w["ew3"], w["ew2"])
    return _combine(pad_start, eidx, pos, base, gate.T, p2d, ys, w, tm_comb)


def _layer_weights(i, w_in, ret_norm_w, diff_norm_w, w_up_ret, w_up_diff, w_out, ln1_w, ln1_b, router_w, router_b,
                   expert_w1, expert_w3, expert_w2, shared_w1, shared_w3, shared_w2, ln2_w, ln2_b, w_pe, w_pg):
    o = _OFFS
    wi = w_in[i]
    vec = lambda a: a[i].reshape(1, -1).astype(F32)
    return {
        "w_mix": jnp.concatenate([wi[:, o[0]:o[3]], wi[:, o[4]:o[7]]], axis=1).astype(BF16),
        "wg": jnp.concatenate([wi[:, o[3]:o[4]], wi[:, o[7]:o[9]]], axis=1).astype(BF16),
        "ret_norm_w": vec(ret_norm_w), "diff_norm_w": vec(diff_norm_w),
        "w_up_ret": w_up_ret[i].astype(BF16), "w_up_diff": w_up_diff[i].astype(BF16), "w_out": w_out[i].astype(BF16),
        "ln1_w": vec(ln1_w), "ln1_b": vec(ln1_b), "ln2_w": vec(ln2_w), "ln2_b": vec(ln2_b),
        "router_wt": router_w[i].T.astype(BF16), "router_b": router_b[i].reshape(-1, 1).astype(F32),
        "ew1": expert_w1[i], "ew3": expert_w3[i], "ew2": expert_w2[i],
        "sw1": shared_w1[i].astype(BF16), "sw3": shared_w3[i].astype(BF16), "sw2": shared_w2[i].astype(BF16),
        "w_pe": w_pe[i].astype(BF16), "w_pg": w_pg[i].astype(BF16),
    }


def _pick(n, pref):
    return pref if n % pref == 0 else n


def kernel(x_prompt, x_sample, cache_diff_k, cache_diff_v, state_retention, page_table, p_prompt, p_sample, w_in, ret_norm_w, diff_lq1, diff_lk1, diff_lq2, diff_lk2, diff_norm_w, w_up_ret, w_up_diff, w_out, ln1_w, ln1_b, router_w, router_b, expert_w1, expert_w3, expert_w2, shared_w1, shared_w3, shared_w2, ln2_w, ln2_b, w_pe, w_pg):
    depth = w_in.shape[0]
    b, s, d = x_prompt.shape
    bd, ls, _ = x_sample.shape
    n_pages = page_table.shape[1]
    past = n_pages * PAGE_SIZE
    alpha = (2 * depth) ** 0.25
    tp, ts = b * s, bd * ls

    cos_p, sin_p = _rotation_tables(jnp.arange(s))
    cos_s, sin_s = _rotation_tables(past + jnp.arange(ls))
    cos_s = jnp.tile(cos_s, (bd, 1))
    sin_s = jnp.tile(sin_s, (bd, 1))

    yp, ys = x_prompt.reshape(tp, d), x_sample.reshape(ts, d)
    kp_l, vp_l, sp_l, ks_l, vs_l, ss_l = [], [], [], [], [], []
    for i in range(depth):
        lambda_init = 0.8 - 0.6 * math.exp(-0.3 * i)
        w = _layer_weights(i, w_in, ret_norm_w, diff_norm_w, w_up_ret, w_up_diff, w_out, ln1_w, ln1_b, router_w,
                           router_b, expert_w1, expert_w3, expert_w2, shared_w1, shared_w3, shared_w2,
                           ln2_w, ln2_b, w_pe, w_pg)
        lam_params = tuple(a[i].reshape(1, DIFF_DK).astype(F32) for a in (diff_lq1, diff_lk1, diff_lq2, diff_lk2))

        rq, rk, rv, dq, dk, dv, dkb, dvb = _proj(yp, w["w_mix"], cos_p, sin_p, _pick(s, 512))
        seq = lambda a: a.reshape(b, s, a.shape[-1])
        on, st_p = _retention(seq(rq), seq(rk), seq(rv), jnp.zeros((b, RET_HEADS, RET_DK, RET_DV), F32),
                              RET_CHUNK, _pick(s, 512))
        od = _diffattn(seq(dq), seq(dkb), seq(dvb), lam_params, w["diff_norm_w"], lambda_init, _pick(s, 512))
        h, base = _tail(yp, on.reshape(tp, RET_V_W), od.reshape(tp, DIFF_V_W), w, alpha, _pick(tp, 256))
        yp = _moe_and_out(h, base, p_prompt[i].reshape(tp, -1), w, _pick(tp, 512), _pick(tp, 256), _pick(tp, 128))
        kp_l.append(dk.reshape(b, s, DIFF_HEADS, 2, DIFF_DK))
        vp_l.append(dv.reshape(b, s, DIFF_HEADS, DIFF_DV))
        sp_l.append(st_p)

        rq, rk, rv, dq, dk, dv, dkb, dvb = _proj(ys, w["w_mix"], cos_s, sin_s, _pick(ts, 512))
        pad_rows = lambda a, r: jnp.pad(a.reshape(bd, ls, a.shape[-1]), ((0, 0), (0, r - ls), (0, 0)))
        on, st_s = _retention(pad_rows(rq, RET_CHUNK), pad_rows(rk, RET_CHUNK), pad_rows(rv, RET_CHUNK),
                              state_retention[i], ls, RET_CHUNK)
        on = on[:, :ls].reshape(ts, RET_V_W)
        q5 = dq.reshape(bd, ls, 1, 2 * DIFF_HEADS, DIFF_DK)
        eye = jnp.eye(2 * DIFF_HEADS, dtype=BF16)[None, None, :, :, None]
        qbd = (q5 * eye).reshape(bd, ls * 2 * DIFF_HEADS, DIFF_QK_W)
        cache_kt = jnp.transpose(cache_diff_k[i], (0, 2, 3, 4, 1)).reshape(-1, DIFF_QK_W, PAGE_SIZE)
        cache_v = cache_diff_v[i].reshape(-1, PAGE_SIZE * DIFF_HEADS, DIFF_DV)
        kt_new = jnp.pad(jnp.swapaxes(dkb.reshape(bd, ls, DIFF_QK_W), 1, 2), ((0, 0), (0, 0), (0, PAGE_SIZE - ls)))
        od = _decode_attn(page_table, cache_kt, cache_v, qbd, kt_new, pad_rows(dvb, PAGE_SIZE), lam_params,
                          w["diff_norm_w"], lambda_init)
        od = od.reshape(ts, DIFF_V_W).astype(BF16)
        h, base = _tail(ys, on, od, w, alpha, _pick(ts, 256))
        ys = _moe_and_out(h, base, p_sample[i].reshape(ts, -1), w, _pick(ts, 512), _pick(ts, 256), _pick(ts, 128))
        ks_l.append(dk.reshape(bd, ls, DIFF_HEADS, 2, DIFF_DK))
        vs_l.append(dv.reshape(bd, ls, DIFF_HEADS, DIFF_DV))
        ss_l.append(st_s)

    return (yp.reshape(b, s, d), ys.reshape(bd, ls, d), jnp.stack(kp_l), jnp.stack(vp_l), jnp.stack(sp_l),
            jnp.stack(ks_l), jnp.stack(vs_l), jnp.stack(ss_l))
```

```python
import functools
import math

import jax
import jax.numpy as jnp
import numpy as np
from jax import lax
from jax.experimental import pallas as pl
from jax.experimental.pallas import tpu as pltpu

F32 = jnp.float32
BF16 = jnp.bfloat16

D_MODEL = 1024
PAGE_SIZE = 128
RET_HEADS = 4
RET_DK = 128
RET_DV = 256
RET_CHUNK = 128
DIFF_HEADS = 4
DIFF_DK = 64
DIFF_DV = 128
N_EXPERTS = 256
TOP_K = 8
N_GROUPS = 8
TOPK_GROUPS = 4
D_EXPERT = 256
ROUTED_SCALE = 2.5
LN_EPS = 1e-5

RET_QK_W = RET_HEADS * RET_DK
RET_V_W = RET_HEADS * RET_DV
DIFF_QK_W = DIFF_HEADS * 2 * DIFF_DK
DIFF_V_W = DIFF_HEADS * DIFF_DV
_SPLITS = (RET_QK_W, RET_QK_W, RET_V_W, RET_V_W, DIFF_QK_W, DIFF_QK_W, DIFF_V_W, D_MODEL, D_MODEL)
_OFFS = tuple(int(o) for o in np.cumsum((0,) + _SPLITS))

LANES = 128
VMEM_LIMIT = 56 * 1024 * 1024

MOE_BLOCK = 256
PAGES_PER_STEP = 8


def _cparams(sem, vmem=VMEM_LIMIT):
    return pltpu.CompilerParams(dimension_semantics=sem, vmem_limit_bytes=vmem)


def _const_spec(shape):
    nd = len(shape)
    return pl.BlockSpec(shape, lambda *_: (0,) * nd)


def _sigmoid(x):
    return 1.0 / (1.0 + jnp.exp(-x))


def _silu(x):
    return x * _sigmoid(x)


PACK_W = D_MODEL // 2
_HI_MASK = np.uint32(0xFFFF0000)


def _pack_row(x):
    bits = lambda v: lax.bitcast_convert_type(v.astype(BF16).astype(F32), jnp.uint32)
    return (bits(x[:, :PACK_W]) >> 16) | (bits(x[:, PACK_W:]) & _HI_MASK)


def _unpack_row(p):
    lo = lax.bitcast_convert_type(p << 16, F32)
    hi = lax.bitcast_convert_type(p & _HI_MASK, F32)
    return jnp.concatenate([lo, hi], axis=1)


def _dot(a, b):
    return jnp.dot(a, b, preferred_element_type=F32)


def _dot_nt(a, b):
    return lax.dot_general(a, b, (((1,), (1,)), ((), ())), preferred_element_type=F32)


def _dot_tn(a, b):
    return lax.dot_general(a, b, (((0,), (0,)), ((), ())), preferred_element_type=F32)


def _swap_pairs(x):
    lane = lax.broadcasted_iota(jnp.int32, x.shape, 1)
    nxt = pltpu.roll(x, LANES - 1, 1)
    prv = pltpu.roll(x, 1, 1)
    return jnp.where((lane & 1) == 0, nxt, prv)


def _proj_kernel(x_ref, w_ref, cos_ref, sin_ref,
                 rq_ref, rk_ref, rv_ref, dq_ref, dk_ref, dv_ref, dkb_ref, dvb_ref):
    xb = x_ref[...].astype(BF16)
    cos = cos_ref[...]
    sin = sin_ref[...]

    def mm(lo, hi):
        return _dot(xb, w_ref[:, lo:hi])

    q = mm(0, 512)
    k = mm(512, 1024)
    for h in range(RET_HEADS):
        sl = slice(h * RET_DK, (h + 1) * RET_DK)
        qh = q[:, sl]
        kh = k[:, sl]
        rq_ref[:, sl] = (qh * cos + _swap_pairs(qh) * sin).astype(BF16)
        rk_ref[:, sl] = ((kh * cos + _swap_pairs(kh) * sin) * (RET_DK ** -0.5)).astype(BF16)
    rv_ref[...] = mm(1024, 2048).astype(BF16)
    dq_ref[...] = (mm(2048, 2560) * (DIFF_DK ** -0.5)).astype(BF16)
    dk = mm(2560, 3072)
    dk_ref[...] = dk
    dkb_ref[...] = dk.astype(BF16)
    dv = mm(3072, 3584)
    dv_ref[...] = dv
    dvb_ref[...] = dv.astype(BF16)


def _proj(x2d, w_mix, cos_t, sin_t, tm):
    t = x2d.shape[0]
    nt = cos_t.shape[0] // tm
    row = lambda w: pl.BlockSpec((tm, w), lambda i: (i, 0))
    tab = pl.BlockSpec((tm, LANES), lambda i: (i % nt, 0))
    out_shapes = (
        jax.ShapeDtypeStruct((t, RET_QK_W), BF16), jax.ShapeDtypeStruct((t, RET_QK_W), BF16),
        jax.ShapeDtypeStruct((t, RET_V_W), BF16), jax.ShapeDtypeStruct((t, DIFF_QK_W), BF16),
        jax.ShapeDtypeStruct((t, DIFF_QK_W), F32), jax.ShapeDtypeStruct((t, DIFF_V_W), F32),
        jax.ShapeDtypeStruct((t, DIFF_QK_W), BF16), jax.ShapeDtypeStruct((t, DIFF_V_W), BF16),
    )
    return pl.pallas_call(
        _proj_kernel,
        out_shape=out_shapes,
        grid=(t // tm,),
        in_specs=[row(D_MODEL), _const_spec(w_mix.shape), tab, tab],
        out_specs=(row(512), row(512), row(1024), row(512), row(512), row(512), row(512), row(512)),
        compiler_params=_cparams(("parallel",)),
        name="proj",
    )(x2d, w_mix, cos_t, sin_t)


def _rotation_tables(pos):
    inv = 1.0 / (10000.0 ** jnp.linspace(0.0, 1.0, RET_DK // 2))
    ang = pos.astype(F32)[:, None] * inv[None, :]
    cos = jnp.repeat(jnp.cos(ang), 2, axis=1)
    sin = jnp.sin(ang)
    sin = jnp.stack([-sin, sin], axis=-1).reshape(pos.shape[0], RET_DK)
    return cos, sin


def _retention_kernel(q_ref, k_ref, v_ref, s0_ref, dm_ref, qd_ref, kd_ref, o_ref, s_ref, st_ref,
                      *, n_sub, gl):
    c = pl.program_id(1)

    @pl.when(c == 0)
    def _():
        st_ref[...] = s0_ref[0]

    for j in range(n_sub):
        rows = slice(j * RET_CHUNK, (j + 1) * RET_CHUNK)
        for h in range(RET_HEADS):
            q = q_ref[0, rows, h * RET_DK:(h + 1) * RET_DK]
            k = k_ref[0, rows, h * RET_DK:(h + 1) * RET_DK]
            v = v_ref[0, rows, h * RET_DV:(h + 1) * RET_DV]
            st = st_ref[h]
            qk = _dot_nt(q, k) * dm_ref[h]
            q_dec = (q.astype(F32) * qd_ref[h]).astype(BF16)
            o = _dot(qk.astype(BF16), v) + _dot(q_dec, st.astype(BF16))
            k_dec = (k.astype(F32) * kd_ref[h]).astype(BF16)
            st_ref[h] = gl[h] * st + _dot_tn(k_dec, v)
            mu = jnp.mean(o, axis=-1, keepdims=True)
            oc = o - mu
            var = jnp.mean(oc * oc, axis=-1, keepdims=True)
            o_ref[0, rows, h * RET_DV:(h + 1) * RET_DV] = (oc * lax.rsqrt(var + LN_EPS)).astype(BF16)

    @pl.when(c == pl.num_programs(1) - 1)
    def _():
        s_ref[0] = st_ref[...]


def _retention_tables(length):
    lg = np.log1p(-(2.0 ** (-5.0 - np.arange(RET_HEADS, dtype=np.float64))))
    idx = np.arange(RET_CHUNK, dtype=np.float64)
    rel = idx[:, None] - idx[None, :]
    valid = (idx < length)
    dm = np.where((rel >= 0) & valid[:, None] & valid[None, :],
                  np.exp(np.maximum(rel, 0.0)[None] * lg[:, None, None]), 0.0)
    qd = np.exp((idx + 1.0)[None, :] * lg[:, None])
    kd = np.where(valid[None, :], np.exp((length - 1.0 - idx)[None, :] * lg[:, None]), 0.0)
    bc = lambda a: np.ascontiguousarray(np.broadcast_to(a[:, :, None], (RET_HEADS, RET_CHUNK, RET_DK)))
    gl = tuple(float(np.exp(length * g)) for g in lg)
    return (jnp.asarray(dm, F32), jnp.asarray(bc(qd), F32), jnp.asarray(bc(kd), F32)), gl


def _retention(rq, rk, rv, state0, length, lb):
    b, s, _ = rq.shape
    (dm, qd, kd), gl = _retention_tables(length)
    seq = lambda w: pl.BlockSpec((1, lb, w), lambda i, c: (i, c, 0))
    st_spec = pl.BlockSpec((1, RET_HEADS, RET_DK, RET_DV), lambda i, c: (i, 0, 0, 0))
    return pl.pallas_call(
        functools.partial(_retention_kernel, n_sub=lb // RET_CHUNK, gl=gl),
        out_shape=(jax.ShapeDtypeStruct((b, s, RET_V_W), BF16),
                   jax.ShapeDtypeStruct((b, RET_HEADS, RET_DK, RET_DV), F32)),
        grid=(b, s // lb),
        in_specs=[seq(RET_QK_W), seq(RET_QK_W), seq(RET_V_W), st_spec,
                  _const_spec(dm.shape), _const_spec(qd.shape), _const_spec(kd.shape)],
        out_specs=(seq(RET_V_W), st_spec),
        scratch_shapes=[pltpu.VMEM((RET_HEADS, RET_DK, RET_DV), F32)],
        compiler_params=_cparams(("parallel", "arbitrary")),
        name="retention",
    )(rq, rk, rv, state0, dm, qd, kd)


def _diff_lambda(lq1_ref, lk1_ref, lq2_ref, lk2_ref, lambda_init):
    a = jnp.sum(lq1_ref[...] * lk1_ref[...], axis=-1, keepdims=True)
    b = jnp.sum(lq2_ref[...] * lk2_ref[...], axis=-1, keepdims=True)
    return jnp.exp(a) - jnp.exp(b) + lambda_init


def _rms_head(o, nw, lambda_init):
    ms = jnp.mean(o * o, axis=-1, keepdims=True)
    return o * lax.rsqrt(ms + LN_EPS) * nw * (1.0 - lambda_init)


def _diffattn_kernel(q_ref, k_ref, v_ref, lq1_ref, lk1_ref, lq2_ref, lk2_ref, nw_ref, o_ref,
                     *, bq, lambda_init):
    qi = pl.program_id(2)
    q = q_ref[0]
    lane = lax.broadcasted_iota(jnp.int32, q.shape, 1)
    zero = jnp.zeros_like(q)
    qq = jnp.concatenate([jnp.where(lane < DIFF_DK, q, zero), jnp.where(lane >= DIFF_DK, q, zero)], axis=0)

    def step(j, carry, masked):
        m, l, acc = carry
        off = pl.multiple_of(j * bq, bq)
        kb = k_ref[0, pl.ds(off, bq), :]
        vb = v_ref[0, pl.ds(off, bq), :]
        s = _dot_nt(qq, kb)
        if masked:
            r = lax.broadcasted_iota(jnp.int32, s.shape, 0)
            r = jnp.where(r >= bq, r - bq, r)
            cidx = lax.broadcasted_iota(jnp.int32, s.shape, 1)
            s = jnp.where(cidx <= r, s, -jnp.inf)
        m_new = jnp.maximum(m, jnp.max(s, axis=-1, keepdims=True))
        p = jnp.exp(s - m_new)
        alpha = jnp.exp(m - m_new)
        l = alpha * l + jnp.sum(p, axis=-1, keepdims=True)
        acc = alpha * acc + _dot(p.astype(BF16), vb)
        return m_new, l, acc

    init = (jnp.full((2 * bq, 1), -jnp.inf, F32), jnp.zeros((2 * bq, 1), F32),
            jnp.zeros((2 * bq, DIFF_DV), F32))
    carry = lax.fori_loop(0, qi, functools.partial(step, masked=False), init)
    _, l, acc = step(qi, carry, True)
    on = acc / l
    lam = _diff_lambda(lq1_ref, lk1_ref, lq2_ref, lk2_ref, lambda_init)
    o = on[:bq] - lam * on[bq:]
    o_ref[0] = _rms_head(o, nw_ref[...], lambda_init).astype(o_ref.dtype)


def _diffattn(dq, dkb, dvb, lam_params, norm_w, lambda_init, bq):
    b, s, _ = dq.shape
    vec = _const_spec((1, DIFF_DK))
    return pl.pallas_call(
        functools.partial(_diffattn_kernel, bq=bq, lambda_init=lambda_init),
        out_shape=jax.ShapeDtypeStruct((b, s, DIFF_V_W), BF16),
        grid=(b, DIFF_HEADS, s // bq),
        in_specs=[pl.BlockSpec((1, bq, 2 * DIFF_DK), lambda i, h, j: (i, j, h)),
                  pl.BlockSpec((1, s, 2 * DIFF_DK), lambda i, h, j: (i, 0, h)),
                  pl.BlockSpec((1, s, DIFF_DV), lambda i, h, j: (i, 0, h)),
                  vec, vec, vec, vec, _const_spec((1, DIFF_DV))],
        out_specs=pl.BlockSpec((1, bq, DIFF_DV), lambda i, h, j: (i, j, h)),
        compiler_params=_cparams(("parallel", "parallel", "arbitrary")),
        name="diffattn",
    )(dq, dkb, dvb, *lam_params, norm_w)


def _decode_kernel(pt_ref, *refs, n_steps, lambda_init):
    npg = PAGES_PER_STEP
    k_refs = refs[:npg]
    v_refs = refs[npg:2 * npg]
    (q_ref, kn_ref, vn_ref, lq1_ref, lk1_ref, lq2_ref, lk2_ref, nw_ref,
     o_ref, m_ref, l_ref, acc_ref) = refs[2 * npg:]
    j = pl.program_id(1)

    @pl.when(j == 0)
    def _():
        m_ref[...] = jnp.full(m_ref.shape, -jnp.inf, F32)
        l_ref[...] = jnp.zeros(l_ref.shape, F32)
        acc_ref[...] = jnp.zeros(acc_ref.shape, F32)

    q = q_ref[0]

    def update(kt, pv, mask):
        s = _dot(q, kt)
        if mask is not None:
            s = jnp.where(mask, s, -jnp.inf)
        m = m_ref[...]
        m_new = jnp.maximum(m, jnp.max(s, axis=-1, keepdims=True))
        p = jnp.exp(s - m_new)
        alpha = jnp.exp(m - m_new)
        l_ref[...] = alpha * l_ref[...] + jnp.sum(p, axis=-1, keepdims=True)
        acc_ref[...] = alpha * acc_ref[...] + pv(p.astype(BF16))
        m_ref[...] = m_new

    def pages_pv(p):
        def head_values(h):
            return jnp.concatenate(
                [v_ref[0, pl.ds(h, PAGE_SIZE, stride=DIFF_HEADS), :].astype(BF16) for v_ref in v_refs], axis=0)

        return jnp.concatenate([_dot(p, head_values(h)) for h in range(DIFF_HEADS)], axis=1)

    update(jnp.concatenate([k_ref[0].astype(BF16) for k_ref in k_refs], axis=1), pages_pv, None)

    @pl.when(j == n_steps - 1)
    def _():
        rows = 4 * 8
        r = lax.broadcasted_iota(jnp.int32, (rows, PAGE_SIZE), 0)
        cidx = lax.broadcasted_iota(jnp.int32, (rows, PAGE_SIZE), 1)
        update(kn_ref[0], lambda p: _dot(p, vn_ref[0]), cidx <= (r >> 3))
        lam = _diff_lambda(lq1_ref, lk1_ref, lq2_ref, lk2_ref, lambda_init)
        on = acc_ref[...] / l_ref[...]
        rr = lax.broadcasted_iota(jnp.int32, on.shape, 0)
        cc = lax.broadcasted_iota(jnp.int32, on.shape, 1)
        g = rr & 7
        coef = jnp.where((g & 1) == 0, 1.0, -lam)
        w = jnp.where((cc >> 7) == (g >> 1), coef, 0.0)
        o = jnp.sum((on * w).reshape(4, 8, DIFF_V_W), axis=1)
        for h in range(DIFF_HEADS):
            sl = slice(h * DIFF_DV, (h + 1) * DIFF_DV)
            o_ref[0, :, sl] = _rms_head(o[:, sl], nw_ref[...], lambda_init)


def _decode_attn(page_table, cache_kt, cache_v, qbd, kt_new, v_new, lam_params, norm_w, lambda_init):
    bd, n_pages = page_table.shape
    npg = PAGES_PER_STEP
    n_steps = n_pages // npg

    def page_spec(i):
        return pl.BlockSpec((1, DIFF_QK_W, PAGE_SIZE), lambda b, j, pt: (pt[b, j * npg + i], 0, 0))

    per_seq = lambda r, c: pl.BlockSpec((1, r, c), lambda b, j, pt: (b, 0, 0))
    vec = pl.BlockSpec((1, DIFF_DK), lambda b, j, pt: (0, 0))
    grid_spec = pltpu.PrefetchScalarGridSpec(
        num_scalar_prefetch=1,
        grid=(bd, n_steps),
        in_specs=[page_spec(i) for i in range(npg)] + [page_spec(i) for i in range(npg)]
        + [per_seq(32, DIFF_QK_W), per_seq(DIFF_QK_W, PAGE_SIZE), per_seq(PAGE_SIZE, DIFF_V_W), vec, vec, vec, vec,
           pl.BlockSpec((1, DIFF_DV), lambda b, j, pt: (0, 0))],
        out_specs=per_seq(4, DIFF_V_W),
        scratch_shapes=[pltpu.VMEM((32, 1), F32), pltpu.VMEM((32, 1), F32), pltpu.VMEM((32, DIFF_V_W), F32)],
    )
    return pl.pallas_call(
        functools.partial(_decode_kernel, n_steps=n_steps, lambda_init=lambda_init),
        out_shape=jax.ShapeDtypeStruct((bd, 4, DIFF_V_W), F32),
        grid_spec=grid_spec,
        compiler_params=_cparams(("parallel", "arbitrary")),
        name="decode_attn",
    )(page_table, *([cache_kt] * npg), *([cache_v] * npg), qbd, kt_new, v_new, *lam_params, norm_w)


def _layer_norm(x, w, b):
    mu = jnp.mean(x, axis=-1, keepdims=True)
    xc = x - mu
    var = jnp.mean(xc * xc, axis=-1, keepdims=True)
    return xc * lax.rsqrt(var + LN_EPS) * w + b


def _tail_kernel(x_ref, on_ref, od_ref, wg_ref, rnw_ref, wur_ref, wud_ref, wo_ref, l1w_ref, l1b_ref,
                 sw1_ref, sw3_ref, sw2_ref, h_ref, base_ref, *, alpha):
    x = x_ref[...]
    xb = x.astype(BF16)
    rg = _dot(xb, wg_ref[:, 0:1024])
    o_ret = _silu(rg) * (on_ref[...].astype(F32) * rnw_ref[...])
    u = _sigmoid(_dot(xb, wg_ref[:, 1024:2048])) * _dot(o_ret.astype(BF16), wur_ref[...])
    u = u + _sigmoid(_dot(xb, wg_ref[:, 2048:3072])) * _dot(od_ref[...], wud_ref[...])
    h = _layer_norm(alpha * x + _dot(u.astype(BF16), wo_ref[...]), l1w_ref[...], l1b_ref[...])
    hb = h.astype(BF16)
    a = _silu(_dot(hb, sw1_ref[...])) * _dot(hb, sw3_ref[...])
    h_ref[...] = _pack_row(h)
    base_ref[...] = alpha * h + _dot(a.astype(BF16), sw2_ref[...])


def _tail(x2d, on, od, w, alpha, tm):
    t = x2d.shape[0]
    row = lambda wd: pl.BlockSpec((tm, wd), lambda i: (i, 0))
    names = ("wg", "ret_norm_w", "w_up_ret", "w_up_diff", "w_out", "ln1_w", "ln1_b", "sw1", "sw3", "sw2")
    ws = [w[n] for n in names]
    return pl.pallas_call(
        functools.partial(_tail_kernel, alpha=alpha),
        out_shape=(jax.ShapeDtypeStruct((t, PACK_W), jnp.uint32), jax.ShapeDtypeStruct((t, D_MODEL), F32)),
        grid=(t // tm,),
        in_specs=[row(D_MODEL), row(RET_V_W), row(DIFF_V_W)] + [_const_spec(a.shape) for a in ws],
        out_specs=(row(PACK_W), row(D_MODEL)),
        compiler_params=_cparams(("parallel",)),
        name="tail",
    )(x2d, on, od, *ws)


def _first_argmax(x, iota, size):
    m = jnp.max(x, axis=0, keepdims=True)
    idx = jnp.min(jnp.where(x == m, iota, size), axis=0, keepdims=True)
    return m, idx


def _router_kernel(h_ref, rw_ref, rb_ref, eidx_ref, gate_ref, pos_ref, cnt_ref, carry_ref):
    i = pl.program_id(0)
    tb = h_ref.shape[0]
    per = N_EXPERTS // N_GROUPS

    @pl.when(i == 0)
    def _():
        carry_ref[...] = jnp.zeros(carry_ref.shape, F32)

    s = _sigmoid(_dot_nt(rw_ref[...], _unpack_row(h_ref[...]).astype(BF16)))
    sb = s + rb_ref[...]
    neg = -jnp.inf

    sb3 = sb.reshape(N_GROUPS, per, tb)
    io3 = lax.broadcasted_iota(jnp.int32, sb3.shape, 1)
    m1 = jnp.max(sb3, axis=1, keepdims=True)
    i1 = jnp.min(jnp.where(sb3 == m1, io3, per), axis=1, keepdims=True)
    m2 = jnp.max(jnp.where(io3 == i1, neg, sb3), axis=1, keepdims=True)
    gscore = (m1 + m2).reshape(N_GROUPS, tb)

    iog = lax.broadcasted_iota(jnp.int32, gscore.shape, 0)
    gsel = jnp.zeros(gscore.shape, F32)
    for _ in range(TOPK_GROUPS):
        _, gi = _first_argmax(gscore, iog, N_GROUPS)
        hit = iog == gi
        gsel = jnp.where(hit, 1.0, gsel)
        gscore = jnp.where(hit, neg, gscore)

    emask = jnp.broadcast_to(gsel.reshape(N_GROUPS, 1, tb), (N_GROUPS, per, tb)).reshape(N_EXPERTS, tb)
    cand = jnp.where(emask > 0.0, sb, neg)
    ioe = lax.broadcasted_iota(jnp.int32, cand.shape, 0)
    chosen = jnp.zeros(cand.shape, F32)
    idxs, gates = [], []
    for _ in range(TOP_K):
        _, ei = _first_argmax(cand, ioe, N_EXPERTS)
        hit = ioe == ei
        gates.append(jnp.sum(jnp.where(hit, s, 0.0), axis=0, keepdims=True))
        idxs.append(ei)
        chosen = jnp.where(hit, 1.0, chosen)
        cand = jnp.where(hit, neg, cand)

    gsum = gates[0]
    for g in gates[1:]:
        gsum = gsum + g
    scale = ROUTED_SCALE / gsum
    for k in range(TOP_K):
        eidx_ref[k:k + 1, :] = idxs[k]
        gate_ref[k:k + 1, :] = gates[k] * scale

    r = lax.broadcasted_iota(jnp.int32, (tb, tb), 0)
    cidx = lax.broadcasted_iota(jnp.int32, (tb, tb), 1)
    before = jnp.where(r < cidx, 1.0, 0.0).astype(BF16)
    rank = carry_ref[...] + _dot(chosen.astype(BF16), before)
    for k in range(TOP_K):
        pk = jnp.sum(jnp.where(ioe == idxs[k], rank, 0.0), axis=0, keepdims=True)
        pos_ref[k:k + 1, :] = pk.astype(jnp.int32)
    total = carry_ref[...] + jnp.sum(chosen, axis=1, keepdims=True)
    carry_ref[...] = total
    cnt_ref[...] = total.astype(jnp.int32)


def _router(h, rw_t, rb_col, tb):
    t = h.shape[0]
    slot = pl.BlockSpec((TOP_K, tb), lambda i: (0, i))
    return pl.pallas_call(
        _router_kernel,
        out_shape=(jax.ShapeDtypeStruct((TOP_K, t), jnp.int32), jax.ShapeDtypeStruct((TOP_K, t), F32),
                   jax.ShapeDtypeStruct((TOP_K, t), jnp.int32), jax.ShapeDtypeStruct((N_EXPERTS, 1), jnp.int32)),
        grid=(t // tb,),
        in_specs=[pl.BlockSpec((tb, PACK_W), lambda i: (i, 0)), _const_spec(rw_t.shape), _const_spec(rb_col.shape)],
        out_specs=(slot, slot, slot, _const_spec((N_EXPERTS, 1))),
        scratch_shapes=[pltpu.VMEM((N_EXPERTS, 1), F32)],
        compiler_params=_cparams(("arbitrary",)),
        name="router",
    )(h, rw_t, rb_col)


def _row_copy_wait(src_rows, dst_rows, sem):
    pltpu.make_async_copy(src_rows, dst_rows, sem).wait()


SUBLANES = 8
_PAD_CHUNKS = tuple(1 << s for s in reversed(range(3, MOE_BLOCK.bit_length() - 1)))


def _zero_fill_padding(ps_ref, cnt_ref, zero_ref, xs_ref, sem):
    def chunks(e, act):
        cnt = cnt_ref[e]
        n_pad = (MOE_BLOCK - (cnt & (MOE_BLOCK - 1))) & (MOE_BLOCK - 1)
        off = ps_ref[e] + cnt
        n_single = n_pad & (SUBLANES - 1)
        for j in range(SUBLANES - 1):
            @pl.when(j < n_single)
            def _(j=j):
                act(pltpu.make_async_copy(zero_ref.at[pl.ds(0, 1), :], xs_ref.at[pl.ds(off + j, 1), :], sem))

        off = off + n_single
        for rows in _PAD_CHUNKS:
            take = n_pad & rows

            @pl.when(take != 0)
            def _(off=off, rows=rows):
                dst = xs_ref.at[pl.ds(pl.multiple_of(off, SUBLANES), rows), :]
                act(pltpu.make_async_copy(zero_ref.at[pl.ds(0, rows), :], dst, sem))

            off = off + take

    def start(e, carry):
        chunks(e, lambda cp: cp.start())
        return carry

    def wait(e, carry):
        chunks(e, lambda cp: cp.wait())
        return carry

    lax.fori_loop(0, N_EXPERTS, start, 0)
    lax.fori_loop(0, N_EXPERTS, wait, 0)


def _zero_fill_unused_blocks(nb_ref, zero_ref, xs_ref, sem):
    rows = zero_ref.shape[0]
    n_total = xs_ref.shape[0] // MOE_BLOCK

    def copies(blk, act):
        for j in range(MOE_BLOCK // rows):
            off = pl.multiple_of(blk * MOE_BLOCK + j * rows, rows)
            act(pltpu.make_async_copy(zero_ref, xs_ref.at[pl.ds(off, rows), :], sem))

    def start(blk, carry):
        copies(blk, lambda cp: cp.start())
        return carry

    def wait(blk, carry):
        copies(blk, lambda cp: cp.wait())
        return carry

    lax.fori_loop(nb_ref[0], n_total, start, 0)
    lax.fori_loop(nb_ref[0], n_total, wait, 0)


def _dest_kernel(ps_ref, eidx_ref, pos_ref, dest_ref):
    ps = ps_ref[...]
    ioe = lax.broadcasted_iota(jnp.int32, (N_EXPERTS, eidx_ref.shape[1]), 0)
    for k in range(TOP_K):
        start = jnp.sum(jnp.where(ioe == eidx_ref[k:k + 1, :], ps, 0.0), axis=0, keepdims=True)
        dest_ref[k:k + 1, :] = start.astype(jnp.int32) + pos_ref[k:k + 1, :]


def _dest_rows(pad_start, eidx, pos, tb):
    t = eidx.shape[1]
    slot = pl.BlockSpec((TOP_K, tb), lambda i: (0, i))
    return pl.pallas_call(
        _dest_kernel,
        out_shape=jax.ShapeDtypeStruct((TOP_K, t), jnp.int32),
        grid=(t // tb,),
        in_specs=[_const_spec((N_EXPERTS, 1)), slot, slot],
        out_specs=slot,
        compiler_params=_cparams(("parallel",)),
        name="dest_rows",
    )(pad_start.astype(F32).reshape(N_EXPERTS, 1), eidx, pos)


def _dispatch_kernel(ps_ref, cnt_ref, nb_ref, dest_ref, h_ref, xs_ref, zero_ref, sem, zsem):
    tm = h_ref.shape[0]

    @pl.when(pl.program_id(0) == 0)
    def _():
        zero_ref[...] = jnp.zeros(zero_ref.shape, zero_ref.dtype)
        _zero_fill_padding(ps_ref, cnt_ref, zero_ref, xs_ref, zsem)
        _zero_fill_unused_blocks(nb_ref, zero_ref, xs_ref, zsem)

    def body(t, carry):
        for k in range(TOP_K):
            pltpu.make_async_copy(h_ref.at[pl.ds(t, 1), :], xs_ref.at[pl.ds(dest_ref[k, t], 1), :], sem).start()
        return carry

    lax.fori_loop(0, tm, body, 0)
    for k in range(TOP_K):
        _row_copy_wait(h_ref, xs_ref.at[pl.ds(0, tm), :], sem)


def _dispatch(pad_start, counts, n_used, dest, h, n_rows, tm):
    t = h.shape[0]
    slot = pl.BlockSpec((TOP_K, tm), lambda i, ps, cnt, nb: (0, i), memory_space=pltpu.SMEM)
    grid_spec = pltpu.PrefetchScalarGridSpec(
        num_scalar_prefetch=3,
        grid=(t // tm,),
        in_specs=[slot, pl.BlockSpec((tm, PACK_W), lambda i, ps, cnt, nb: (i, 0))],
        out_specs=pl.BlockSpec(memory_space=pl.ANY),
        scratch_shapes=[pltpu.VMEM((_PAD_CHUNKS[0], PACK_W), jnp.uint32), pltpu.SemaphoreType.DMA(()),
                        pltpu.SemaphoreType.DMA(())],
    )
    return pl.pallas_call(
        _dispatch_kernel,
        out_shape=jax.ShapeDtypeStruct((n_rows, PACK_W), jnp.uint32),
        grid_spec=grid_spec,
        compiler_params=_cparams(("arbitrary",)),
        name="dispatch",
    )(pad_start, counts, n_used, dest, h)


def _expert_kernel(be_ref, nb_ref, xs_ref, w1_ref, w3_ref, w2_ref, ys_ref, w1b_ref, w3b_ref, w2b_ref):
    i = pl.program_id(0)

    @pl.when(jnp.logical_or(i == 0, be_ref[i] != be_ref[jnp.maximum(i - 1, 0)]))
    def _():
        w1b_ref[...] = w1_ref[0].astype(BF16)
        w3b_ref[...] = w3_ref[0].astype(BF16)
        w2b_ref[...] = w2_ref[0].astype(BF16)

    @pl.when(i < nb_ref[0])
    def _():
        xb = _unpack_row(xs_ref[...]).astype(BF16)
        a = _silu(_dot(xb, w1b_ref[...])) * _dot(xb, w3b_ref[...])
        ys_ref[...] = _pack_row(_dot(a.astype(BF16), w2b_ref[...]))

    @pl.when(i >= nb_ref[0])
    def _():
        ys_ref[...] = jnp.zeros(ys_ref.shape, ys_ref.dtype)


def _experts(block_expert, n_used, xs, w1, w3, w2):
    n_rows = xs.shape[0]
    grid_spec = pltpu.PrefetchScalarGridSpec(
        num_scalar_prefetch=2,
        grid=(n_rows // MOE_BLOCK,),
        in_specs=[pl.BlockSpec((MOE_BLOCK, PACK_W), lambda i, be, nb: (jnp.minimum(i, nb[0] - 1), 0)),
                  pl.BlockSpec((1, D_MODEL, D_EXPERT), lambda i, be, nb: (be[i], 0, 0)),
                  pl.BlockSpec((1, D_MODEL, D_EXPERT), lambda i, be, nb: (be[i], 0, 0)),
                  pl.BlockSpec((1, D_EXPERT, D_MODEL), lambda i, be, nb: (be[i], 0, 0))],
        out_specs=pl.BlockSpec((MOE_BLOCK, PACK_W), lambda i, be, nb: (i, 0)),
        scratch_shapes=[pltpu.VMEM((D_MODEL, D_EXPERT), BF16), pltpu.VMEM((D_MODEL, D_EXPERT), BF16),
                        pltpu.VMEM((D_EXPERT, D_MODEL), BF16)],
    )
    return pl.pallas_call(
        _expert_kernel,
        out_shape=jax.ShapeDtypeStruct((n_rows, PACK_W), jnp.uint32),
        grid_spec=grid_spec,
        compiler_params=_cparams(("arbitrary",)),
        name="experts",
    )(block_expert, n_used, xs, w1, w3, w2)


def _combine_kernel(dcur_ref, dnxt_ref, base_ref, gate_ref, p_ref, ys_ref,
                    l2w_ref, l2b_ref, wpg_ref, wpe_ref, y_ref, rows_ref, sems):
    i = pl.program_id(0)
    n = pl.num_programs(0)
    tm = base_ref.shape[0]

    def start_gather(d_ref, slot):
        def body(t, carry):
            for k in range(TOP_K):
                pltpu.make_async_copy(ys_ref.at[pl.ds(d_ref[k, t], 1), :],
                                      rows_ref.at[slot, k, pl.ds(t, 1), :], sems.at[slot]).start()
            return carry

        lax.fori_loop(0, tm, body, 0)

    @pl.when(i == 0)
    def _():
        start_gather(dcur_ref, 0)

    slot = i % 2

    @pl.when(i + 1 < n)
    def _():
        start_gather(dnxt_ref, 1 - slot)

    for k in range(TOP_K):
        _row_copy_wait(ys_ref.at[pl.ds(0, tm), :], rows_ref.at[slot, k], sems.at[slot])

    g = gate_ref[...]
    m = base_ref[...]
    for k in range(TOP_K):
        m = m + _unpack_row(rows_ref[slot, k]) * g[:, k:k + 1]
    h2 = _layer_norm(m, l2w_ref[...], l2b_ref[...])
    emb = _dot(p_ref[...].astype(BF16), wpe_ref[...])
    y_ref[...] = h2 + _sigmoid(_dot(h2.astype(BF16), wpg_ref[...])) * emb


def _combine(dest, base, gate_t, p2d, ys, w, tm):
    t = base.shape[0]
    n = t // tm
    row = lambda wd: pl.BlockSpec((tm, wd), lambda i: (i, 0))
    cur = pl.BlockSpec((TOP_K, tm), lambda i: (0, i), memory_space=pltpu.SMEM)
    nxt = pl.BlockSpec((TOP_K, tm), lambda i: (0, jnp.minimum(i + 1, n - 1)), memory_space=pltpu.SMEM)
    ws = [w[nm] for nm in ("ln2_w", "ln2_b", "w_pg", "w_pe")]
    return pl.pallas_call(
        _combine_kernel,
        out_shape=jax.ShapeDtypeStruct((t, D_MODEL), F32),
        grid=(n,),
        in_specs=[cur, nxt, row(D_MODEL), row(TOP_K), row(p2d.shape[1]), pl.BlockSpec(memory_space=pl.ANY)]
        + [_const_spec(a.shape) for a in ws],
        out_specs=row(D_MODEL),
        scratch_shapes=[pltpu.VMEM((2, TOP_K, tm, PACK_W), jnp.uint32), pltpu.SemaphoreType.DMA((2,))],
        compiler_params=_cparams(("arbitrary",)),
        name="combine",
    )(dest, dest, base, gate_t, p2d, ys, *ws)


def _moe_and_out(h, base, p2d, w, tb, tm_disp, tm_comb):
    t = h.shape[0]
    eidx, gate, pos, counts = _router(h, w["router_wt"], w["router_b"], tb)
    counts = counts[:, 0]
    padded = (counts + MOE_BLOCK - 1) // MOE_BLOCK * MOE_BLOCK
    pad_end = jnp.cumsum(padded)
    pad_start = pad_end - padded
    n_blocks = (t * TOP_K + N_EXPERTS * (MOE_BLOCK - 1) + MOE_BLOCK - 1) // MOE_BLOCK
    block_expert = jnp.minimum(
        jnp.searchsorted(pad_end, jnp.arange(n_blocks, dtype=jnp.int32) * MOE_BLOCK, side="right"),
        N_EXPERTS - 1).astype(jnp.int32)
    n_used = (pad_end[-1:] // MOE_BLOCK).astype(jnp.int32)
    dest = _dest_rows(pad_start, eidx, pos, tb)
    xs = _dispatch(pad_start, counts, n_used, dest, h, n_blocks * MOE_BLOCK, tm_disp)
    ys = _experts(block_expert, n_used, xs, w["ew1"], w["ew3"], w["ew2"])
    return _combine(dest, base, gate.T, p2d, ys, w, tm_comb)


def _layer_weights(i, w_in, ret_norm_w, diff_norm_w, w_up_ret, w_up_diff, w_out, ln1_w, ln1_b, router_w, router_b,
                   expert_w1, expert_w3, expert_w2, shared_w1, shared_w3, shared_w2, ln2_w, ln2_b, w_pe, w_pg):
    o = _OFFS
    wi = w_in[i]
    vec = lambda a: a[i].reshape(1, -1).astype(F32)
    return {
        "w_mix": jnp.concatenate([wi[:, o[0]:o[3]], wi[:, o[4]:o[7]]], axis=1).astype(BF16),
        "wg": jnp.concatenate([wi[:, o[3]:o[4]], wi[:, o[7]:o[9]]], axis=1).astype(BF16),
        "ret_norm_w": vec(ret_norm_w), "diff_norm_w": vec(diff_norm_w),
        "w_up_ret": w_up_ret[i].astype(BF16), "w_up_diff": w_up_diff[i].astype(BF16), "w_out": w_out[i].astype(BF16),
        "ln1_w": vec(ln1_w), "ln1_b": vec(ln1_b), "ln2_w": vec(ln2_w), "ln2_b": vec(ln2_b),
        "router_wt": router_w[i].T.astype(BF16), "router_b": router_b[i].reshape(-1, 1).astype(F32),
        "ew1": expert_w1[i], "ew3": expert_w3[i], "ew2": expert_w2[i],
        "sw1": shared_w1[i].astype(BF16), "sw3": shared_w3[i].astype(BF16), "sw2": shared_w2[i].astype(BF16),
        "w_pe": w_pe[i].astype(BF16), "w_pg": w_pg[i].astype(BF16),
    }


def _pick(n, pref):
    return pref if n % pref == 0 else n


def kernel(x_prompt, x_sample, cache_diff_k, cache_diff_v, state_retention, page_table, p_prompt, p_sample, w_in, ret_norm_w, diff_lq1, diff_lk1, diff_lq2, diff_lk2, diff_norm_w, w_up_ret, w_up_diff, w_out, ln1_w, ln1_b, router_w, router_b, expert_w1, expert_w3, expert_w2, shared_w1, shared_w3, shared_w2, ln2_w, ln2_b, w_pe, w_pg):
    depth = w_in.shape[0]
    b, s, d = x_prompt.shape
    bd, ls, _ = x_sample.shape
    n_pages = page_table.shape[1]
    past = n_pages * PAGE_SIZE
    alpha = (2 * depth) ** 0.25
    tp, ts = b * s, bd * ls

    cos_p, sin_p = _rotation_tables(jnp.arange(s))
    cos_s, sin_s = _rotation_tables(past + jnp.arange(ls))
    cos_s = jnp.tile(cos_s, (bd, 1))
    sin_s = jnp.tile(sin_s, (bd, 1))

    yp, ys = x_prompt.reshape(tp, d), x_sample.reshape(ts, d)
    kp_l, vp_l, sp_l, ks_l, vs_l, ss_l = [], [], [], [], [], []
    for i in range(depth):
        lambda_init = 0.8 - 0.6 * math.exp(-0.3 * i)
        w = _layer_weights(i, w_in, ret_norm_w, diff_norm_w, w_up_ret, w_up_diff, w_out, ln1_w, ln1_b, router_w,
                           router_b, expert_w1, expert_w3, expert_w2, shared_w1, shared_w3, shared_w2,
                           ln2_w, ln2_b, w_pe, w_pg)
        lam_params = tuple(a[i].reshape(1, DIFF_DK).astype(F32) for a in (diff_lq1, diff_lk1, diff_lq2, diff_lk2))

        rq, rk, rv, dq, dk, dv, dkb, dvb = _proj(yp, w["w_mix"], cos_p, sin_p, _pick(s, 512))
        seq = lambda a: a.reshape(b, s, a.shape[-1])
        on, st_p = _retention(seq(rq), seq(rk), seq(rv), jnp.zeros((b, RET_HEADS, RET_DK, RET_DV), F32),
                              RET_CHUNK, _pick(s, 512))
        od = _diffattn(seq(dq), seq(dkb), seq(dvb), lam_params, w["diff_norm_w"], lambda_init, _pick(s, 512))
        h, base = _tail(yp, on.reshape(tp, RET_V_W), od.reshape(tp, DIFF_V_W), w, alpha, _pick(tp, 256))
        yp = _moe_and_out(h, base, p_prompt[i].reshape(tp, -1), w, _pick(tp, 512), _pick(tp, 256), _pick(tp, 128))
        kp_l.append(dk.reshape(b, s, DIFF_HEADS, 2, DIFF_DK))
        vp_l.append(dv.reshape(b, s, DIFF_HEADS, DIFF_DV))
        sp_l.append(st_p)

        rq, rk, rv, dq, dk, dv, dkb, dvb = _proj(ys, w["w_mix"], cos_s, sin_s, _pick(ts, 512))
        pad_rows = lambda a, r: jnp.pad(a.reshape(bd, ls, a.shape[-1]), ((0, 0), (0, r - ls), (0, 0)))
        on, st_s = _retention(pad_rows(rq, RET_CHUNK), pad_rows(rk, RET_CHUNK), pad_rows(rv, RET_CHUNK),
                              state_retention[i], ls, RET_CHUNK)
        on = on[:, :ls].reshape(ts, RET_V_W)
        q5 = dq.reshape(bd, ls, 1, 2 * DIFF_HEADS, DIFF_DK)
        eye = jnp.eye(2 * DIFF_HEADS, dtype=BF16)[None, None, :, :, None]
        qbd = (q5 * eye).reshape(bd, ls * 2 * DIFF_HEADS, DIFF_QK_W)
        cache_kt = jnp.transpose(cache_diff_k[i], (0, 2, 3, 4, 1)).reshape(-1, DIFF_QK_W, PAGE_SIZE)
        cache_v = cache_diff_v[i].reshape(-1, PAGE_SIZE * DIFF_HEADS, DIFF_DV)
        kt_new = jnp.pad(jnp.swapaxes(dkb.reshape(bd, ls, DIFF_QK_W), 1, 2), ((0, 0), (0, 0), (0, PAGE_SIZE - ls)))
        od = _decode_attn(page_table, cache_kt, cache_v, qbd, kt_new, pad_rows(dvb, PAGE_SIZE), lam_params,
                          w["diff_norm_w"], lambda_init)
        od = od.reshape(ts, DIFF_V_W).astype(BF16)
        h, base = _tail(ys, on, od, w, alpha, _pick(ts, 256))
        ys = _moe_and_out(h, base, p_sample[i].reshape(ts, -1), w, _pick(ts, 512), _pick(ts, 256), _pick(ts, 128))
        ks_l.append(dk.reshape(bd, ls, DIFF_HEADS, 2, DIFF_DK))
        vs_l.append(dv.reshape(bd, ls, DIFF_HEADS, DIFF_DV))
        ss_l.append(st_s)

    return (yp.reshape(b, s, d), ys.reshape(bd, ls, d), jnp.stack(kp_l), jnp.stack(vp_l), jnp.stack(sp_l),
            jnp.stack(ks_l), jnp.stack(vs_l), jnp.stack(ss_l))
```

```python
import functools
import math

import jax
import jax.numpy as jnp
import numpy as np
from jax import lax
from jax.experimental import pallas as pl
from jax.experimental.pallas import tpu as pltpu

F32 = jnp.float32
BF16 = jnp.bfloat16

D_MODEL = 1024
PAGE_SIZE = 128
RET_HEADS = 4
RET_DK = 128
RET_DV = 256
RET_CHUNK = 128
DIFF_HEADS = 4
DIFF_DK = 64
DIFF_DV = 128
N_EXPERTS = 256
TOP_K = 8
N_GROUPS = 8
TOPK_GROUPS = 4
D_EXPERT = 256
ROUTED_SCALE = 2.5
LN_EPS = 1e-5

RET_QK_W = RET_HEADS * RET_DK
RET_V_W = RET_HEADS * RET_DV
DIFF_QK_W = DIFF_HEADS * 2 * DIFF_DK
DIFF_V_W = DIFF_HEADS * DIFF_DV
_SPLITS = (RET_QK_W, RET_QK_W, RET_V_W, RET_V_W, DIFF_QK_W, DIFF_QK_W, DIFF_V_W, D_MODEL, D_MODEL)
_OFFS = tuple(int(o) for o in np.cumsum((0,) + _SPLITS))

LANES = 128
VMEM_LIMIT = 56 * 1024 * 1024

MOE_BLOCK = 256
PAGES_PER_STEP = 16
ROW_COPY_UNROLL = 4


def _cparams(sem, vmem=VMEM_LIMIT):
    return pltpu.CompilerParams(dimension_semantics=sem, vmem_limit_bytes=vmem)


def _const_spec(shape):
    nd = len(shape)
    return pl.BlockSpec(shape, lambda *_: (0,) * nd)


def _sigmoid(x):
    return 1.0 / (1.0 + jnp.exp(-x))


def _silu(x):
    return x * _sigmoid(x)


PACK_W = D_MODEL // 2
_HI_MASK = np.uint32(0xFFFF0000)


def _pack_row(x):
    bits = lambda v: lax.bitcast_convert_type(v.astype(BF16).astype(F32), jnp.uint32)
    return (bits(x[:, :PACK_W]) >> 16) | (bits(x[:, PACK_W:]) & _HI_MASK)


def _unpack_row(p):
    lo = lax.bitcast_convert_type(p << 16, F32)
    hi = lax.bitcast_convert_type(p & _HI_MASK, F32)
    return jnp.concatenate([lo, hi], axis=1)


def _dot(a, b):
    return jnp.dot(a, b, preferred_element_type=F32)


def _dot_nt(a, b):
    return lax.dot_general(a, b, (((1,), (1,)), ((), ())), preferred_element_type=F32)


def _dot_tn(a, b):
    return lax.dot_general(a, b, (((0,), (0,)), ((), ())), preferred_element_type=F32)


def _swap_pairs(x):
    lane = lax.broadcasted_iota(jnp.int32, x.shape, 1)
    nxt = pltpu.roll(x, LANES - 1, 1)
    prv = pltpu.roll(x, 1, 1)
    return jnp.where((lane & 1) == 0, nxt, prv)


def _proj_kernel(x_ref, w_ref, cos_ref, sin_ref,
                 rq_ref, rk_ref, rv_ref, dq_ref, dk_ref, dv_ref, dkb_ref, dvb_ref):
    xb = x_ref[...].astype(BF16)
    cos = cos_ref[...]
    sin = sin_ref[...]

    def mm(lo, hi):
        return _dot(xb, w_ref[:, lo:hi])

    q = mm(0, 512)
    k = mm(512, 1024)
    for h in range(RET_HEADS):
        sl = slice(h * RET_DK, (h + 1) * RET_DK)
        qh = q[:, sl]
        kh = k[:, sl]
        rq_ref[:, sl] = (qh * cos + _swap_pairs(qh) * sin).astype(BF16)
        rk_ref[:, sl] = ((kh * cos + _swap_pairs(kh) * sin) * (RET_DK ** -0.5)).astype(BF16)
    rv_ref[...] = mm(1024, 2048).astype(BF16)
    dq_ref[...] = (mm(2048, 2560) * (DIFF_DK ** -0.5)).astype(BF16)
    tm = x_ref.shape[0]
    dk = mm(2560, 3072)
    for g in range(2 * DIFF_HEADS):
        dk_ref[pl.ds(g, tm, stride=2 * DIFF_HEADS), :] = dk[:, g * DIFF_DK:(g + 1) * DIFF_DK]
    dkb_ref[...] = dk.astype(BF16)
    dv = mm(3072, 3584)
    for h in range(DIFF_HEADS):
        dv_ref[pl.ds(h, tm, stride=DIFF_HEADS), :] = dv[:, h * DIFF_DV:(h + 1) * DIFF_DV]
    dvb_ref[...] = dv.astype(BF16)


def _proj(x2d, w_mix, cos_t, sin_t, tm):
    t = x2d.shape[0]
    nt = cos_t.shape[0] // tm
    row = lambda w: pl.BlockSpec((tm, w), lambda i: (i, 0))
    tab = pl.BlockSpec((tm, LANES), lambda i: (i % nt, 0))
    out_shapes = (
        jax.ShapeDtypeStruct((t, RET_QK_W), BF16), jax.ShapeDtypeStruct((t, RET_QK_W), BF16),
        jax.ShapeDtypeStruct((t, RET_V_W), BF16), jax.ShapeDtypeStruct((t, DIFF_QK_W), BF16),
        jax.ShapeDtypeStruct((t * 2 * DIFF_HEADS, DIFF_DK), F32), jax.ShapeDtypeStruct((t * DIFF_HEADS, DIFF_DV), F32),
        jax.ShapeDtypeStruct((t, DIFF_QK_W), BF16), jax.ShapeDtypeStruct((t, DIFF_V_W), BF16),
    )
    k_rows = pl.BlockSpec((tm * 2 * DIFF_HEADS, DIFF_DK), lambda i: (i, 0))
    v_rows = pl.BlockSpec((tm * DIFF_HEADS, DIFF_DV), lambda i: (i, 0))
    return pl.pallas_call(
        _proj_kernel,
        out_shape=out_shapes,
        grid=(t // tm,),
        in_specs=[row(D_MODEL), _const_spec(w_mix.shape), tab, tab],
        out_specs=(row(512), row(512), row(1024), row(512), k_rows, v_rows, row(512), row(512)),
        compiler_params=_cparams(("parallel",)),
        name="proj",
    )(x2d, w_mix, cos_t, sin_t)


def _rotation_tables(pos):
    inv = 1.0 / (10000.0 ** jnp.linspace(0.0, 1.0, RET_DK // 2))
    ang = pos.astype(F32)[:, None] * inv[None, :]
    cos = jnp.repeat(jnp.cos(ang), 2, axis=1)
    sin = jnp.sin(ang)
    sin = jnp.stack([-sin, sin], axis=-1).reshape(pos.shape[0], RET_DK)
    return cos, sin


def _retention_kernel(q_ref, k_ref, v_ref, s0_ref, dm_ref, qd_ref, kd_ref, o_ref, s_ref, st_ref,
                      *, n_sub, gl):
    c = pl.program_id(1)

    @pl.when(c == 0)
    def _():
        st_ref[...] = s0_ref[0]

    for j in range(n_sub):
        rows = slice(j * RET_CHUNK, (j + 1) * RET_CHUNK)
        for h in range(RET_HEADS):
            q = q_ref[0, rows, h * RET_DK:(h + 1) * RET_DK]
            k = k_ref[0, rows, h * RET_DK:(h + 1) * RET_DK]
            v = v_ref[0, rows, h * RET_DV:(h + 1) * RET_DV]
            st = st_ref[h]
            qk = _dot_nt(q, k) * dm_ref[h]
            q_dec = (q.astype(F32) * qd_ref[h]).astype(BF16)
            o = _dot(qk.astype(BF16), v) + _dot(q_dec, st.astype(BF16))
            k_dec = (k.astype(F32) * kd_ref[h]).astype(BF16)
            st_ref[h] = gl[h] * st + _dot_tn(k_dec, v)
            mu = jnp.mean(o, axis=-1, keepdims=True)
            oc = o - mu
            var = jnp.mean(oc * oc, axis=-1, keepdims=True)
            o_ref[0, rows, h * RET_DV:(h + 1) * RET_DV] = (oc * lax.rsqrt(var + LN_EPS)).astype(BF16)

    @pl.when(c == pl.num_programs(1) - 1)
    def _():
        s_ref[0] = st_ref[...]


def _retention_tables(length):
    lg = np.log1p(-(2.0 ** (-5.0 - np.arange(RET_HEADS, dtype=np.float64))))
    idx = np.arange(RET_CHUNK, dtype=np.float64)
    rel = idx[:, None] - idx[None, :]
    valid = (idx < length)
    dm = np.where((rel >= 0) & valid[:, None] & valid[None, :],
                  np.exp(np.maximum(rel, 0.0)[None] * lg[:, None, None]), 0.0)
    qd = np.exp((idx + 1.0)[None, :] * lg[:, None])
    kd = np.where(valid[None, :], np.exp((length - 1.0 - idx)[None, :] * lg[:, None]), 0.0)
    bc = lambda a: np.ascontiguousarray(np.broadcast_to(a[:, :, None], (RET_HEADS, RET_CHUNK, RET_DK)))
    gl = tuple(float(np.exp(length * g)) for g in lg)
    return (jnp.asarray(dm, F32), jnp.asarray(bc(qd), F32), jnp.asarray(bc(kd), F32)), gl


def _retention(rq, rk, rv, state0, length, lb):
    b, s, _ = rq.shape
    (dm, qd, kd), gl = _retention_tables(length)
    seq = lambda w: pl.BlockSpec((1, lb, w), lambda i, c: (i, c, 0))
    st_spec = pl.BlockSpec((1, RET_HEADS, RET_DK, RET_DV), lambda i, c: (i, 0, 0, 0))
    return pl.pallas_call(
        functools.partial(_retention_kernel, n_sub=lb // RET_CHUNK, gl=gl),
        out_shape=(jax.ShapeDtypeStruct((b, s, RET_V_W), BF16),
                   jax.ShapeDtypeStruct((b, RET_HEADS, RET_DK, RET_DV), F32)),
        grid=(b, s // lb),
        in_specs=[seq(RET_QK_W), seq(RET_QK_W), seq(RET_V_W), st_spec,
                  _const_spec(dm.shape), _const_spec(qd.shape), _const_spec(kd.shape)],
        out_specs=(seq(RET_V_W), st_spec),
        scratch_shapes=[pltpu.VMEM((RET_HEADS, RET_DK, RET_DV), F32)],
        compiler_params=_cparams(("parallel", "arbitrary")),
        name="retention",
    )(rq, rk, rv, state0, dm, qd, kd)


def _diff_lambda(lq1_ref, lk1_ref, lq2_ref, lk2_ref, lambda_init):
    a = jnp.sum(lq1_ref[...] * lk1_ref[...], axis=-1, keepdims=True)
    b = jnp.sum(lq2_ref[...] * lk2_ref[...], axis=-1, keepdims=True)
    return jnp.exp(a) - jnp.exp(b) + lambda_init


def _rms_head(o, nw, lambda_init):
    ms = jnp.mean(o * o, axis=-1, keepdims=True)
    return o * lax.rsqrt(ms + LN_EPS) * nw * (1.0 - lambda_init)


def _diffattn_kernel(q_ref, k_ref, v_ref, lq1_ref, lk1_ref, lq2_ref, lk2_ref, nw_ref, o_ref,
                     *, bq, lambda_init):
    qi = pl.program_id(2)
    q = q_ref[0]
    lane = lax.broadcasted_iota(jnp.int32, q.shape, 1)
    zero = jnp.zeros_like(q)
    qq = jnp.concatenate([jnp.where(lane < DIFF_DK, q, zero), jnp.where(lane >= DIFF_DK, q, zero)], axis=0)

    def step(j, carry, masked):
        m, l, acc = carry
        off = pl.multiple_of(j * bq, bq)
        kb = k_ref[0, pl.ds(off, bq), :]
        vb = v_ref[0, pl.ds(off, bq), :]
        s = _dot_nt(qq, kb)
        if masked:
            r = lax.broadcasted_iota(jnp.int32, s.shape, 0)
            r = jnp.where(r >= bq, r - bq, r)
            cidx = lax.broadcasted_iota(jnp.int32, s.shape, 1)
            s = jnp.where(cidx <= r, s, -jnp.inf)
        m_new = jnp.maximum(m, jnp.max(s, axis=-1, keepdims=True))
        p = jnp.exp(s - m_new)
        alpha = jnp.exp(m - m_new)
        l = alpha * l + jnp.sum(p, axis=-1, keepdims=True)
        acc = alpha * acc + _dot(p.astype(BF16), vb)
        return m_new, l, acc

    init = (jnp.full((2 * bq, 1), -jnp.inf, F32), jnp.zeros((2 * bq, 1), F32),
            jnp.zeros((2 * bq, DIFF_DV), F32))
    carry = lax.fori_loop(0, qi, functools.partial(step, masked=False), init)
    _, l, acc = step(qi, carry, True)
    on = acc / l
    lam = _diff_lambda(lq1_ref, lk1_ref, lq2_ref, lk2_ref, lambda_init)
    o = on[:bq] - lam * on[bq:]
    o_ref[0] = _rms_head(o, nw_ref[...], lambda_init).astype(o_ref.dtype)


def _diffattn(dq, dkb, dvb, lam_params, norm_w, lambda_init, bq):
    b, s, _ = dq.shape
    vec = _const_spec((1, DIFF_DK))
    return pl.pallas_call(
        functools.partial(_diffattn_kernel, bq=bq, lambda_init=lambda_init),
        out_shape=jax.ShapeDtypeStruct((b, s, DIFF_V_W), BF16),
        grid=(b, DIFF_HEADS, s // bq),
        in_specs=[pl.BlockSpec((1, bq, 2 * DIFF_DK), lambda i, h, j: (i, j, h)),
                  pl.BlockSpec((1, s, 2 * DIFF_DK), lambda i, h, j: (i, 0, h)),
                  pl.BlockSpec((1, s, DIFF_DV), lambda i, h, j: (i, 0, h)),
                  vec, vec, vec, vec, _const_spec((1, DIFF_DV))],
        out_specs=pl.BlockSpec((1, bq, DIFF_DV), lambda i, h, j: (i, j, h)),
        compiler_params=_cparams(("parallel", "parallel", "arbitrary")),
        name="diffattn",
    )(dq, dkb, dvb, *lam_params, norm_w)


def _decode_kernel(pt_ref, *refs, npg, n_steps, lambda_init):
    k_refs = refs[:npg]
    v_refs = refs[npg:2 * npg]
    (q_ref, kn_ref, vn_ref, lq1_ref, lk1_ref, lq2_ref, lk2_ref, nw_ref,
     o_ref, m_ref, l_ref, acc_ref) = refs[2 * npg:]
    j = pl.program_id(1)

    @pl.when(j == 0)
    def _():
        m_ref[...] = jnp.full(m_ref.shape, -jnp.inf, F32)
        l_ref[...] = jnp.zeros(l_ref.shape, F32)
        acc_ref[...] = jnp.zeros(acc_ref.shape, F32)

    q = q_ref[0]

    def update(kt, pv, mask):
        s = _dot(q, kt)
        if mask is not None:
            s = jnp.where(mask, s, -jnp.inf)
        m = m_ref[...]
        m_new = jnp.maximum(m, jnp.max(s, axis=-1, keepdims=True))
        p = jnp.exp(s - m_new)
        alpha = jnp.exp(m - m_new)
        l_ref[...] = alpha * l_ref[...] + jnp.sum(p, axis=-1, keepdims=True)
        acc_ref[...] = alpha * acc_ref[...] + pv(p.astype(BF16))
        m_ref[...] = m_new

    def pages_pv(p):
        def head_values(h):
            return jnp.concatenate(
                [v_ref[0, pl.ds(h, PAGE_SIZE, stride=DIFF_HEADS), :].astype(BF16) for v_ref in v_refs], axis=0)

        return jnp.concatenate([_dot(p, head_values(h)) for h in range(DIFF_HEADS)], axis=1)

    update(jnp.concatenate([k_ref[0].astype(BF16) for k_ref in k_refs], axis=1), pages_pv, None)

    @pl.when(j == n_steps - 1)
    def _():
        rows = 4 * 8
        r = lax.broadcasted_iota(jnp.int32, (rows, PAGE_SIZE), 0)
        cidx = lax.broadcasted_iota(jnp.int32, (rows, PAGE_SIZE), 1)
        update(kn_ref[0], lambda p: _dot(p, vn_ref[0]), cidx <= (r >> 3))
        lam = _diff_lambda(lq1_ref, lk1_ref, lq2_ref, lk2_ref, lambda_init)
        on = acc_ref[...] / l_ref[...]
        rr = lax.broadcasted_iota(jnp.int32, on.shape, 0)
        cc = lax.broadcasted_iota(jnp.int32, on.shape, 1)
        g = rr & 7
        coef = jnp.where((g & 1) == 0, 1.0, -lam)
        w = jnp.where((cc >> 7) == (g >> 1), coef, 0.0)
        o = jnp.sum((on * w).reshape(4, 8, DIFF_V_W), axis=1)
        for h in range(DIFF_HEADS):
            sl = slice(h * DIFF_DV, (h + 1) * DIFF_DV)
            o_ref[0, :, sl] = _rms_head(o[:, sl], nw_ref[...], lambda_init)


def _decode_attn(page_table, cache_kt, cache_v, qbd, kt_new, v_new, lam_params, norm_w, lambda_init):
    bd, n_pages = page_table.shape
    npg = math.gcd(n_pages, PAGES_PER_STEP)
    n_steps = n_pages // npg

    def page_spec(i):
        return pl.BlockSpec((1, DIFF_QK_W, PAGE_SIZE), lambda b, j, pt: (pt[b, j * npg + i], 0, 0))

    per_seq = lambda r, c: pl.BlockSpec((1, r, c), lambda b, j, pt: (b, 0, 0))
    vec = pl.BlockSpec((1, DIFF_DK), lambda b, j, pt: (0, 0))
    grid_spec = pltpu.PrefetchScalarGridSpec(
        num_scalar_prefetch=1,
        grid=(bd, n_steps),
        in_specs=[page_spec(i) for i in range(npg)] + [page_spec(i) for i in range(npg)]
        + [per_seq(32, DIFF_QK_W), per_seq(DIFF_QK_W, PAGE_SIZE), per_seq(PAGE_SIZE, DIFF_V_W), vec, vec, vec, vec,
           pl.BlockSpec((1, DIFF_DV), lambda b, j, pt: (0, 0))],
        out_specs=per_seq(4, DIFF_V_W),
        scratch_shapes=[pltpu.VMEM((32, 1), F32), pltpu.VMEM((32, 1), F32), pltpu.VMEM((32, DIFF_V_W), F32)],
    )
    return pl.pallas_call(
        functools.partial(_decode_kernel, npg=npg, n_steps=n_steps, lambda_init=lambda_init),
        out_shape=jax.ShapeDtypeStruct((bd, 4, DIFF_V_W), F32),
        grid_spec=grid_spec,
        compiler_params=_cparams(("parallel", "arbitrary")),
        name="decode_attn",
    )(page_table, *([cache_kt] * npg), *([cache_v] * npg), qbd, kt_new, v_new, *lam_params, norm_w)


def _layer_norm(x, w, b):
    mu = jnp.mean(x, axis=-1, keepdims=True)
    xc = x - mu
    var = jnp.mean(xc * xc, axis=-1, keepdims=True)
    return xc * lax.rsqrt(var + LN_EPS) * w + b


def _tail_kernel(x_ref, on_ref, od_ref, wg_ref, rnw_ref, wur_ref, wud_ref, wo_ref, l1w_ref, l1b_ref,
                 sw1_ref, sw3_ref, sw2_ref, h_ref, base_ref, *, alpha):
    x = x_ref[...]
    xb = x.astype(BF16)
    rg = _dot(xb, wg_ref[:, 0:1024])
    o_ret = _silu(rg) * (on_ref[...].astype(F32) * rnw_ref[...])
    u = _sigmoid(_dot(xb, wg_ref[:, 1024:2048])) * _dot(o_ret.astype(BF16), wur_ref[...])
    u = u + _sigmoid(_dot(xb, wg_ref[:, 2048:3072])) * _dot(od_ref[...], wud_ref[...])
    h = _layer_norm(alpha * x + _dot(u.astype(BF16), wo_ref[...]), l1w_ref[...], l1b_ref[...])
    hb = h.astype(BF16)
    a = _silu(_dot(hb, sw1_ref[...])) * _dot(hb, sw3_ref[...])
    h_ref[...] = _pack_row(h)
    base_ref[...] = alpha * h + _dot(a.astype(BF16), sw2_ref[...])


def _tail(x2d, on, od, w, alpha, tm):
    t = x2d.shape[0]
    row = lambda wd: pl.BlockSpec((tm, wd), lambda i: (i, 0))
    names = ("wg", "ret_norm_w", "w_up_ret", "w_up_diff", "w_out", "ln1_w", "ln1_b", "sw1", "sw3", "sw2")
    ws = [w[n] for n in names]
    return pl.pallas_call(
        functools.partial(_tail_kernel, alpha=alpha),
        out_shape=(jax.ShapeDtypeStruct((t, PACK_W), jnp.uint32), jax.ShapeDtypeStruct((t, D_MODEL), F32)),
        grid=(t // tm,),
        in_specs=[row(D_MODEL), row(RET_V_W), row(DIFF_V_W)] + [_const_spec(a.shape) for a in ws],
        out_specs=(row(PACK_W), row(D_MODEL)),
        compiler_params=_cparams(("parallel",)),
        name="tail",
    )(x2d, on, od, *ws)


def _first_argmax(x, iota, size):
    m = jnp.max(x, axis=0, keepdims=True)
    idx = jnp.min(jnp.where(x == m, iota, size), axis=0, keepdims=True)
    return m, idx


def _router_kernel(h_ref, rw_ref, rb_ref, eidx_ref, gate_ref, pos_ref, cnt_ref, carry_ref):
    i = pl.program_id(0)
    tb = h_ref.shape[0]
    per = N_EXPERTS // N_GROUPS

    @pl.when(i == 0)
    def _():
        carry_ref[...] = jnp.zeros(carry_ref.shape, F32)

    s = _sigmoid(_dot_nt(rw_ref[...], _unpack_row(h_ref[...]).astype(BF16)))
    sb = s + rb_ref[...]
    neg = -jnp.inf

    sb3 = sb.reshape(N_GROUPS, per, tb)
    io3 = lax.broadcasted_iota(jnp.int32, sb3.shape, 1)
    m1 = jnp.max(sb3, axis=1, keepdims=True)
    i1 = jnp.min(jnp.where(sb3 == m1, io3, per), axis=1, keepdims=True)
    m2 = jnp.max(jnp.where(io3 == i1, neg, sb3), axis=1, keepdims=True)
    gscore = (m1 + m2).reshape(N_GROUPS, tb)

    iog = lax.broadcasted_iota(jnp.int32, gscore.shape, 0)
    gsel = jnp.zeros(gscore.shape, F32)
    for _ in range(TOPK_GROUPS):
        _, gi = _first_argmax(gscore, iog, N_GROUPS)
        hit = iog == gi
        gsel = jnp.where(hit, 1.0, gsel)
        gscore = jnp.where(hit, neg, gscore)

    emask = jnp.broadcast_to(gsel.reshape(N_GROUPS, 1, tb), (N_GROUPS, per, tb)).reshape(N_EXPERTS, tb)
    cand = jnp.where(emask > 0.0, sb, neg)
    ioe = lax.broadcasted_iota(jnp.int32, cand.shape, 0)
    chosen = jnp.zeros(cand.shape, F32)
    idxs, gates = [], []
    for _ in range(TOP_K):
        _, ei = _first_argmax(cand, ioe, N_EXPERTS)
        hit = ioe == ei
        gates.append(jnp.sum(jnp.where(hit, s, 0.0), axis=0, keepdims=True))
        idxs.append(ei)
        chosen = jnp.where(hit, 1.0, chosen)
        cand = jnp.where(hit, neg, cand)

    gsum = gates[0]
    for g in gates[1:]:
        gsum = gsum + g
    scale = ROUTED_SCALE / gsum
    for k in range(TOP_K):
        eidx_ref[k:k + 1, :] = idxs[k]
        gate_ref[k:k + 1, :] = gates[k] * scale

    r = lax.broadcasted_iota(jnp.int32, (tb, tb), 0)
    cidx = lax.broadcasted_iota(jnp.int32, (tb, tb), 1)
    before = jnp.where(r < cidx, 1.0, 0.0).astype(BF16)
    rank = carry_ref[...] + _dot(chosen.astype(BF16), before)
    for k in range(TOP_K):
        pk = jnp.sum(jnp.where(ioe == idxs[k], rank, 0.0), axis=0, keepdims=True)
        pos_ref[k:k + 1, :] = pk.astype(jnp.int32)
    total = carry_ref[...] + jnp.sum(chosen, axis=1, keepdims=True)
    carry_ref[...] = total
    cnt_ref[...] = total.astype(jnp.int32)


def _router(h, rw_t, rb_col, tb):
    t = h.shape[0]
    slot = pl.BlockSpec((TOP_K, tb), lambda i: (0, i))
    return pl.pallas_call(
        _router_kernel,
        out_shape=(jax.ShapeDtypeStruct((TOP_K, t), jnp.int32), jax.ShapeDtypeStruct((TOP_K, t), F32),
                   jax.ShapeDtypeStruct((TOP_K, t), jnp.int32), jax.ShapeDtypeStruct((N_EXPERTS, 1), jnp.int32)),
        grid=(t // tb,),
        in_specs=[pl.BlockSpec((tb, PACK_W), lambda i: (i, 0)), _const_spec(rw_t.shape), _const_spec(rb_col.shape)],
        out_specs=(slot, slot, slot, _const_spec((N_EXPERTS, 1))),
        scratch_shapes=[pltpu.VMEM((N_EXPERTS, 1), F32)],
        compiler_params=_cparams(("arbitrary",)),
        name="router",
    )(h, rw_t, rb_col)


def _row_copy_wait(src_rows, dst_rows, sem):
    pltpu.make_async_copy(src_rows, dst_rows, sem).wait()


SUBLANES = 8
_PAD_CHUNKS = tuple(1 << s for s in reversed(range(3, MOE_BLOCK.bit_length() - 1)))


def _zero_fill_padding(ps_ref, cnt_ref, zero_ref, xs_ref, sem):
    def chunks(e, act):
        cnt = cnt_ref[e]
        n_pad = (MOE_BLOCK - (cnt & (MOE_BLOCK - 1))) & (MOE_BLOCK - 1)
        off = ps_ref[e] + cnt
        n_single = n_pad & (SUBLANES - 1)
        for j in range(SUBLANES - 1):
            @pl.when(j < n_single)
            def _(j=j):
                act(pltpu.make_async_copy(zero_ref.at[pl.ds(0, 1), :], xs_ref.at[pl.ds(off + j, 1), :], sem))

        off = off + n_single
        for rows in _PAD_CHUNKS:
            take = n_pad & rows

            @pl.when(take != 0)
            def _(off=off, rows=rows):
                dst = xs_ref.at[pl.ds(pl.multiple_of(off, SUBLANES), rows), :]
                act(pltpu.make_async_copy(zero_ref.at[pl.ds(0, rows), :], dst, sem))

            off = off + take

    def start(e, carry):
        chunks(e, lambda cp: cp.start())
        return carry

    def wait(e, carry):
        chunks(e, lambda cp: cp.wait())
        return carry

    lax.fori_loop(0, N_EXPERTS, start, 0)
    lax.fori_loop(0, N_EXPERTS, wait, 0)


def _zero_fill_unused_blocks(nb_ref, zero_ref, xs_ref, sem):
    rows = zero_ref.shape[0]
    n_total = xs_ref.shape[0] // MOE_BLOCK

    def copies(blk, act):
        for j in range(MOE_BLOCK // rows):
            off = pl.multiple_of(blk * MOE_BLOCK + j * rows, rows)
            act(pltpu.make_async_copy(zero_ref, xs_ref.at[pl.ds(off, rows), :], sem))

    def start(blk, carry):
        copies(blk, lambda cp: cp.start())
        return carry

    def wait(blk, carry):
        copies(blk, lambda cp: cp.wait())
        return carry

    lax.fori_loop(nb_ref[0], n_total, start, 0)
    lax.fori_loop(nb_ref[0], n_total, wait, 0)


def _dest_kernel(ps_ref, eidx_ref, pos_ref, dest_ref):
    ps = ps_ref[...]
    ioe = lax.broadcasted_iota(jnp.int32, (N_EXPERTS, eidx_ref.shape[1]), 0)
    for k in range(TOP_K):
        start = jnp.sum(jnp.where(ioe == eidx_ref[k:k + 1, :], ps, 0.0), axis=0, keepdims=True)
        dest_ref[k:k + 1, :] = start.astype(jnp.int32) + pos_ref[k:k + 1, :]


def _dest_rows(pad_start, eidx, pos, tb):
    t = eidx.shape[1]
    slot = pl.BlockSpec((TOP_K, tb), lambda i: (0, i))
    return pl.pallas_call(
        _dest_kernel,
        out_shape=jax.ShapeDtypeStruct((TOP_K, t), jnp.int32),
        grid=(t // tb,),
        in_specs=[_const_spec((N_EXPERTS, 1)), slot, slot],
        out_specs=slot,
        compiler_params=_cparams(("parallel",)),
        name="dest_rows",
    )(pad_start.astype(F32).reshape(N_EXPERTS, 1), eidx, pos)


def _dispatch_kernel(ps_ref, cnt_ref, nb_ref, dest_ref, h_ref, xs_ref, zero_ref, sem, zsem):
    tm = h_ref.shape[0]

    @pl.when(pl.program_id(0) == 0)
    def _():
        zero_ref[...] = jnp.zeros(zero_ref.shape, zero_ref.dtype)
        _zero_fill_padding(ps_ref, cnt_ref, zero_ref, xs_ref, zsem)
        _zero_fill_unused_blocks(nb_ref, zero_ref, xs_ref, zsem)

    def body(t, carry):
        for k in range(TOP_K):
            pltpu.make_async_copy(h_ref.at[pl.ds(t, 1), :], xs_ref.at[pl.ds(dest_ref[k, t], 1), :],
                                  sem).start(priority=k % 2)
        return carry

    lax.fori_loop(0, tm, body, 0, unroll=ROW_COPY_UNROLL)
    for k in range(TOP_K):
        _row_copy_wait(h_ref, xs_ref.at[pl.ds(0, tm), :], sem)


def _dispatch(pad_start, counts, n_used, dest, h, n_rows, tm):
    t = h.shape[0]
    slot = pl.BlockSpec((TOP_K, tm), lambda i, ps, cnt, nb: (0, i), memory_space=pltpu.SMEM)
    grid_spec = pltpu.PrefetchScalarGridSpec(
        num_scalar_prefetch=3,
        grid=(t // tm,),
        in_specs=[slot, pl.BlockSpec((tm, PACK_W), lambda i, ps, cnt, nb: (i, 0))],
        out_specs=pl.BlockSpec(memory_space=pl.ANY),
        scratch_shapes=[pltpu.VMEM((_PAD_CHUNKS[0], PACK_W), jnp.uint32), pltpu.SemaphoreType.DMA(()),
                        pltpu.SemaphoreType.DMA(())],
    )
    return pl.pallas_call(
        _dispatch_kernel,
        out_shape=jax.ShapeDtypeStruct((n_rows, PACK_W), jnp.uint32),
        grid_spec=grid_spec,
        compiler_params=_cparams(("arbitrary",)),
        name="dispatch",
    )(pad_start, counts, n_used, dest, h)


def _expert_kernel(be_ref, nb_ref, xs_ref, w1_ref, w3_ref, w2_ref, ys_ref, w1b_ref, w3b_ref, w2b_ref):
    i = pl.program_id(0)

    @pl.when(jnp.logical_or(i == 0, be_ref[i] != be_ref[jnp.maximum(i - 1, 0)]))
    def _():
        w1b_ref[...] = w1_ref[0].astype(BF16)
        w3b_ref[...] = w3_ref[0].astype(BF16)
        w2b_ref[...] = w2_ref[0].astype(BF16)

    @pl.when(i < nb_ref[0])
    def _():
        xb = _unpack_row(xs_ref[...]).astype(BF16)
        a = _silu(_dot(xb, w1b_ref[...])) * _dot(xb, w3b_ref[...])
        ys_ref[...] = _pack_row(_dot(a.astype(BF16), w2b_ref[...]))

    @pl.when(i >= nb_ref[0])
    def _():
        ys_ref[...] = jnp.zeros(ys_ref.shape, ys_ref.dtype)


def _experts(block_expert, n_used, xs, w1, w3, w2):
    n_rows = xs.shape[0]
    grid_spec = pltpu.PrefetchScalarGridSpec(
        num_scalar_prefetch=2,
        grid=(n_rows // MOE_BLOCK,),
        in_specs=[pl.BlockSpec((MOE_BLOCK, PACK_W), lambda i, be, nb: (jnp.minimum(i, nb[0] - 1), 0)),
                  pl.BlockSpec((1, D_MODEL, D_EXPERT), lambda i, be, nb: (be[i], 0, 0)),
                  pl.BlockSpec((1, D_MODEL, D_EXPERT), lambda i, be, nb: (be[i], 0, 0)),
                  pl.BlockSpec((1, D_EXPERT, D_MODEL), lambda i, be, nb: (be[i], 0, 0))],
        out_specs=pl.BlockSpec((MOE_BLOCK, PACK_W), lambda i, be, nb: (i, 0)),
        scratch_shapes=[pltpu.VMEM((D_MODEL, D_EXPERT), BF16), pltpu.VMEM((D_MODEL, D_EXPERT), BF16),
                        pltpu.VMEM((D_EXPERT, D_MODEL), BF16)],
    )
    return pl.pallas_call(
        _expert_kernel,
        out_shape=jax.ShapeDtypeStruct((n_rows, PACK_W), jnp.uint32),
        grid_spec=grid_spec,
        compiler_params=_cparams(("arbitrary",)),
        name="experts",
    )(block_expert, n_used, xs, w1, w3, w2)


def _combine_kernel(dcur_ref, dnxt_ref, base_ref, gate_ref, p_ref, ys_ref,
                    l2w_ref, l2b_ref, wpg_ref, wpe_ref, y_ref, rows_ref, sems):
    i = pl.program_id(0)
    n = pl.num_programs(0)
    tm = base_ref.shape[0]

    def start_gather(d_ref, slot):
        def body(t, carry):
            for k in range(TOP_K):
                pltpu.make_async_copy(ys_ref.at[pl.ds(d_ref[k, t], 1), :],
                                      rows_ref.at[slot, k, pl.ds(t, 1), :], sems.at[slot]).start(priority=k % 2)
            return carry

        lax.fori_loop(0, tm, body, 0, unroll=ROW_COPY_UNROLL)

    @pl.when(i == 0)
    def _():
        start_gather(dcur_ref, 0)

    slot = i % 2

    @pl.when(i + 1 < n)
    def _():
        start_gather(dnxt_ref, 1 - slot)

    for k in range(TOP_K):
        _row_copy_wait(ys_ref.at[pl.ds(0, tm), :], rows_ref.at[slot, k], sems.at[slot])

    g = gate_ref[...]
    m = base_ref[...]
    for k in range(TOP_K):
        m = m + _unpack_row(rows_ref[slot, k]) * g[:, k:k + 1]
    h2 = _layer_norm(m, l2w_ref[...], l2b_ref[...])
    emb = _dot(p_ref[...].astype(BF16), wpe_ref[...])
    y_ref[...] = h2 + _sigmoid(_dot(h2.astype(BF16), wpg_ref[...])) * emb


def _combine(dest, base, gate_t, p2d, ys, w, tm):
    t = base.shape[0]
    n = t // tm
    row = lambda wd: pl.BlockSpec((tm, wd), lambda i: (i, 0))
    cur = pl.BlockSpec((TOP_K, tm), lambda i: (0, i), memory_space=pltpu.SMEM)
    nxt = pl.BlockSpec((TOP_K, tm), lambda i: (0, jnp.minimum(i + 1, n - 1)), memory_space=pltpu.SMEM)
    ws = [w[nm] for nm in ("ln2_w", "ln2_b", "w_pg", "w_pe")]
    return pl.pallas_call(
        _combine_kernel,
        out_shape=jax.ShapeDtypeStruct((t, D_MODEL), F32),
        grid=(n,),
        in_specs=[cur, nxt, row(D_MODEL), row(TOP_K), row(p2d.shape[1]), pl.BlockSpec(memory_space=pl.ANY)]
        + [_const_spec(a.shape) for a in ws],
        out_specs=row(D_MODEL),
        scratch_shapes=[pltpu.VMEM((2, TOP_K, tm, PACK_W), jnp.uint32), pltpu.SemaphoreType.DMA((2,))],
        compiler_params=_cparams(("arbitrary",)),
        name="combine",
    )(dest, dest, base, gate_t, p2d, ys, *ws)


def _moe_and_out(h, base, p2d, w, tb, tm_disp, tm_comb):
    t = h.shape[0]
    eidx, gate, pos, counts = _router(h, w["router_wt"], w["router_b"], tb)
    counts = counts[:, 0]
    padded = (counts + MOE_BLOCK - 1) // MOE_BLOCK * MOE_BLOCK
    pad_end = jnp.cumsum(padded)
    pad_start = pad_end - padded
    n_blocks = (t * TOP_K + N_EXPERTS * (MOE_BLOCK - 1) + MOE_BLOCK - 1) // MOE_BLOCK
    block_expert = jnp.minimum(
        jnp.searchsorted(pad_end, jnp.arange(n_blocks, dtype=jnp.int32) * MOE_BLOCK, side="right"),
        N_EXPERTS - 1).astype(jnp.int32)
    n_used = (pad_end[-1:] // MOE_BLOCK).astype(jnp.int32)
    dest = _dest_rows(pad_start, eidx, pos, tb)
    xs = _dispatch(pad_start, counts, n_used, dest, h, n_blocks * MOE_BLOCK, tm_disp)
    ys = _experts(block_expert, n_used, xs, w["ew1"], w["ew3"], w["ew2"])
    return _combine(dest, base, gate.T, p2d, ys, w, tm_comb)


def _layer_weights(i, w_in, ret_norm_w, diff_norm_w, w_up_ret, w_up_diff, w_out, ln1_w, ln1_b, router_w, router_b,
                   expert_w1, expert_w3, expert_w2, shared_w1, shared_w3, shared_w2, ln2_w, ln2_b, w_pe, w_pg):
    o = _OFFS
    wi = w_in[i]
    vec = lambda a: a[i].reshape(1, -1).astype(F32)
    return {
        "w_mix": jnp.concatenate([wi[:, o[0]:o[3]], wi[:, o[4]:o[7]]], axis=1).astype(BF16),
        "wg": jnp.concatenate([wi[:, o[3]:o[4]], wi[:, o[7]:o[9]]], axis=1).astype(BF16),
        "ret_norm_w": vec(ret_norm_w), "diff_norm_w": vec(diff_norm_w),
        "w_up_ret": w_up_ret[i].astype(BF16), "w_up_diff": w_up_diff[i].astype(BF16), "w_out": w_out[i].astype(BF16),
        "ln1_w": vec(ln1_w), "ln1_b": vec(ln1_b), "ln2_w": vec(ln2_w), "ln2_b": vec(ln2_b),
        "router_wt": router_w[i].T.astype(BF16), "router_b": router_b[i].reshape(-1, 1).astype(F32),
        "ew1": expert_w1[i], "ew3": expert_w3[i], "ew2": expert_w2[i],
        "sw1": shared_w1[i].astype(BF16), "sw3": shared_w3[i].astype(BF16), "sw2": shared_w2[i].astype(BF16),
        "w_pe": w_pe[i].astype(BF16), "w_pg": w_pg[i].astype(BF16),
    }


def _pick(n, pref):
    return pref if n % pref == 0 else n


def kernel(x_prompt, x_sample, cache_diff_k, cache_diff_v, state_retention, page_table, p_prompt, p_sample, w_in, ret_norm_w, diff_lq1, diff_lk1, diff_lq2, diff_lk2, diff_norm_w, w_up_ret, w_up_diff, w_out, ln1_w, ln1_b, router_w, router_b, expert_w1, expert_w3, expert_w2, shared_w1, shared_w3, shared_w2, ln2_w, ln2_b, w_pe, w_pg):
    depth = w_in.shape[0]
    b, s, d = x_prompt.shape
    bd, ls, _ = x_sample.shape
    n_pages = page_table.shape[1]
    past = n_pages * PAGE_SIZE
    alpha = (2 * depth) ** 0.25
    tp, ts = b * s, bd * ls

    cos_p, sin_p = _rotation_tables(jnp.arange(s))
    cos_s, sin_s = _rotation_tables(past + jnp.arange(ls))
    cos_s = jnp.tile(cos_s, (bd, 1))
    sin_s = jnp.tile(sin_s, (bd, 1))

    yp, ys = x_prompt.reshape(tp, d), x_sample.reshape(ts, d)
    kp_l, vp_l, sp_l, ks_l, vs_l, ss_l = [], [], [], [], [], []
    for i in range(depth):
        lambda_init = 0.8 - 0.6 * math.exp(-0.3 * i)
        w = _layer_weights(i, w_in, ret_norm_w, diff_norm_w, w_up_ret, w_up_diff, w_out, ln1_w, ln1_b, router_w,
                           router_b, expert_w1, expert_w3, expert_w2, shared_w1, shared_w3, shared_w2,
                           ln2_w, ln2_b, w_pe, w_pg)
        lam_params = tuple(a[i].reshape(1, DIFF_DK).astype(F32) for a in (diff_lq1, diff_lk1, diff_lq2, diff_lk2))

        rq, rk, rv, dq, dk, dv, dkb, dvb = _proj(yp, w["w_mix"], cos_p, sin_p, _pick(s, 512))
        seq = lambda a: a.reshape(b, s, a.shape[-1])
        on, st_p = _retention(seq(rq), seq(rk), seq(rv), jnp.zeros((b, RET_HEADS, RET_DK, RET_DV), F32),
                              RET_CHUNK, _pick(s, 512))
        od = _diffattn(seq(dq), seq(dkb), seq(dvb), lam_params, w["diff_norm_w"], lambda_init, _pick(s, 512))
        h, base = _tail(yp, on.reshape(tp, RET_V_W), od.reshape(tp, DIFF_V_W), w, alpha, _pick(tp, 256))
        yp = _moe_and_out(h, base, p_prompt[i].reshape(tp, -1), w, _pick(tp, 512), _pick(tp, 256), _pick(tp, 128))
        kp_l.append(dk.reshape(b, s, DIFF_HEADS, 2, DIFF_DK))
        vp_l.append(dv.reshape(b, s, DIFF_HEADS, DIFF_DV))
        sp_l.append(st_p)

        rq, rk, rv, dq, dk, dv, dkb, dvb = _proj(ys, w["w_mix"], cos_s, sin_s, _pick(ts, 512))
        pad_rows = lambda a, r: jnp.pad(a.reshape(bd, ls, a.shape[-1]), ((0, 0), (0, r - ls), (0, 0)))
        on, st_s = _retention(pad_rows(rq, RET_CHUNK), pad_rows(rk, RET_CHUNK), pad_rows(rv, RET_CHUNK),
                              state_retention[i], ls, RET_CHUNK)
        on = on[:, :ls].reshape(ts, RET_V_W)
        q5 = dq.reshape(bd, ls, 1, 2 * DIFF_HEADS, DIFF_DK)
        eye = jnp.eye(2 * DIFF_HEADS, dtype=BF16)[None, None, :, :, None]
        qbd = (q5 * eye).reshape(bd, ls * 2 * DIFF_HEADS, DIFF_QK_W)
        cache_kt = jnp.transpose(cache_diff_k[i], (0, 2, 3, 4, 1)).reshape(-1, DIFF_QK_W, PAGE_SIZE)
        cache_v = cache_diff_v[i].reshape(-1, PAGE_SIZE * DIFF_HEADS, DIFF_DV)
        kt_new = jnp.pad(jnp.swapaxes(dkb.reshape(bd, ls, DIFF_QK_W), 1, 2), ((0, 0), (0, 0), (0, PAGE_SIZE - ls)))
        od = _decode_attn(page_table, cache_kt, cache_v, qbd, kt_new, pad_rows(dvb, PAGE_SIZE), lam_params,
                          w["diff_norm_w"], lambda_init)
        od = od.reshape(ts, DIFF_V_W).astype(BF16)
        h, base = _tail(ys, on, od, w, alpha, _pick(ts, 256))
        ys = _moe_and_out(h, base, p_sample[i].reshape(ts, -1), w, _pick(ts, 512), _pick(ts, 256), _pick(ts, 128))
        ks_l.append(dk.reshape(bd, ls, DIFF_HEADS, 2, DIFF_DK))
        vs_l.append(dv.reshape(bd, ls, DIFF_HEADS, DIFF_DV))
        ss_l.append(st_s)

    return (yp.reshape(b, s, d), ys.reshape(bd, ls, d), jnp.stack(kp_l), jnp.stack(vp_l), jnp.stack(sp_l),
            jnp.stack(ks_l), jnp.stack(vs_l), jnp.stack(ss_l))
```

```python
import functools
import math

import jax
import jax.numpy as jnp
import numpy as np
from jax import lax
from jax.experimental import pallas as pl
from jax.experimental.pallas import tpu as pltpu

F32 = jnp.float32
BF16 = jnp.bfloat16

D_MODEL = 1024
PAGE_SIZE = 128
RET_HEADS = 4
RET_DK = 128
RET_DV = 256
RET_CHUNK = 128
DIFF_HEADS = 4
DIFF_DK = 64
DIFF_DV = 128
N_EXPERTS = 256
TOP_K = 8
N_GROUPS = 8
TOPK_GROUPS = 4
D_EXPERT = 256
ROUTED_SCALE = 2.5
LN_EPS = 1e-5

RET_QK_W = RET_HEADS * RET_DK
RET_V_W = RET_HEADS * RET_DV
DIFF_QK_W = DIFF_HEADS * 2 * DIFF_DK
DIFF_V_W = DIFF_HEADS * DIFF_DV
_SPLITS = (RET_QK_W, RET_QK_W, RET_V_W, RET_V_W, DIFF_QK_W, DIFF_QK_W, DIFF_V_W, D_MODEL, D_MODEL)
_OFFS = tuple(int(o) for o in np.cumsum((0,) + _SPLITS))

LANES = 128
SUBLANES = 8
VMEM_LIMIT = 56 * 1024 * 1024

MOE_BLOCK = 256
PAGES_PER_STEP = 16


def _cparams(sem, vmem=VMEM_LIMIT):
    return pltpu.CompilerParams(dimension_semantics=sem, vmem_limit_bytes=vmem)


def _const_spec(shape):
    nd = len(shape)
    return pl.BlockSpec(shape, lambda *_: (0,) * nd)


def _sigmoid(x):
    return 1.0 / (1.0 + jnp.exp(-x))


def _silu(x):
    return x * _sigmoid(x)


PACK_W = D_MODEL // 2
ROW_SLABS = PACK_W // LANES
_HI_MASK = np.uint32(0xFFFF0000)


def _pack_row(x):
    bits = lambda v: lax.bitcast_convert_type(v.astype(BF16).astype(F32), jnp.uint32)
    return (bits(x[:, :PACK_W]) >> 16) | (bits(x[:, PACK_W:]) & _HI_MASK)


def _unpack_row(p):
    lo = lax.bitcast_convert_type(p << 16, F32)
    hi = lax.bitcast_convert_type(p & _HI_MASK, F32)
    return jnp.concatenate([lo, hi], axis=1)


def _store_slabs(ref, packed):
    rows = packed.shape[0]
    for j in range(ROW_SLABS):
        ref[pl.ds(j, rows, stride=ROW_SLABS), :] = packed[:, j * LANES:(j + 1) * LANES]


def _load_slabs(ref, rows):
    return jnp.concatenate([ref[pl.ds(j, rows, stride=ROW_SLABS), :] for j in range(ROW_SLABS)], axis=1)


def _slab(ref, t_pair, par):
    tile = ref.at[pl.ds(pl.multiple_of(t_pair * SUBLANES, SUBLANES), SUBLANES)]
    return tile.at[par * ROW_SLABS:(par + 1) * ROW_SLABS]


def _dot(a, b):
    return jnp.dot(a, b, preferred_element_type=F32)


def _dot_nt(a, b):
    return lax.dot_general(a, b, (((1,), (1,)), ((), ())), preferred_element_type=F32)


def _dot_tn(a, b):
    return lax.dot_general(a, b, (((0,), (0,)), ((), ())), preferred_element_type=F32)


def _swap_pairs(x):
    lane = lax.broadcasted_iota(jnp.int32, x.shape, 1)
    nxt = pltpu.roll(x, LANES - 1, 1)
    prv = pltpu.roll(x, 1, 1)
    return jnp.where((lane & 1) == 0, nxt, prv)


def _proj_kernel(x_ref, w_ref, cos_ref, sin_ref,
                 rq_ref, rk_ref, rv_ref, dq_ref, dk_ref, dv_ref, dkb_ref, dvb_ref):
    xb = x_ref[...].astype(BF16)
    cos = cos_ref[...]
    sin = sin_ref[...]

    def mm(lo, hi):
        return _dot(xb, w_ref[:, lo:hi])

    q = mm(0, 512)
    k = mm(512, 1024)
    for h in range(RET_HEADS):
        sl = slice(h * RET_DK, (h + 1) * RET_DK)
        qh = q[:, sl]
        kh = k[:, sl]
        rq_ref[:, sl] = (qh * cos + _swap_pairs(qh) * sin).astype(BF16)
        rk_ref[:, sl] = ((kh * cos + _swap_pairs(kh) * sin) * (RET_DK ** -0.5)).astype(BF16)
    rv_ref[...] = mm(1024, 2048).astype(BF16)
    dq_ref[...] = (mm(2048, 2560) * (DIFF_DK ** -0.5)).astype(BF16)
    tm = x_ref.shape[0]
    dk = mm(2560, 3072)
    for g in range(2 * DIFF_HEADS):
        dk_ref[pl.ds(g, tm, stride=2 * DIFF_HEADS), :] = dk[:, g * DIFF_DK:(g + 1) * DIFF_DK]
    dkb_ref[...] = dk.astype(BF16)
    dv = mm(3072, 3584)
    for h in range(DIFF_HEADS):
        dv_ref[pl.ds(h, tm, stride=DIFF_HEADS), :] = dv[:, h * DIFF_DV:(h + 1) * DIFF_DV]
    dvb_ref[...] = dv.astype(BF16)


def _proj(x2d, w_mix, cos_t, sin_t, tm):
    t = x2d.shape[0]
    nt = cos_t.shape[0] // tm
    row = lambda w: pl.BlockSpec((tm, w), lambda i: (i, 0))
    tab = pl.BlockSpec((tm, LANES), lambda i: (i % nt, 0))
    out_shapes = (
        jax.ShapeDtypeStruct((t, RET_QK_W), BF16), jax.ShapeDtypeStruct((t, RET_QK_W), BF16),
        jax.ShapeDtypeStruct((t, RET_V_W), BF16), jax.ShapeDtypeStruct((t, DIFF_QK_W), BF16),
        jax.ShapeDtypeStruct((t * 2 * DIFF_HEADS, DIFF_DK), F32), jax.ShapeDtypeStruct((t * DIFF_HEADS, DIFF_DV), F32),
        jax.ShapeDtypeStruct((t, DIFF_QK_W), BF16), jax.ShapeDtypeStruct((t, DIFF_V_W), BF16),
    )
    k_rows = pl.BlockSpec((tm * 2 * DIFF_HEADS, DIFF_DK), lambda i: (i, 0))
    v_rows = pl.BlockSpec((tm * DIFF_HEADS, DIFF_DV), lambda i: (i, 0))
    return pl.pallas_call(
        _proj_kernel,
        out_shape=out_shapes,
        grid=(t // tm,),
        in_specs=[row(D_MODEL), _const_spec(w_mix.shape), tab, tab],
        out_specs=(row(512), row(512), row(1024), row(512), k_rows, v_rows, row(512), row(512)),
        compiler_params=_cparams(("parallel",)),
        name="proj",
    )(x2d, w_mix, cos_t, sin_t)


def _rotation_tables(pos):
    inv = 1.0 / (10000.0 ** jnp.linspace(0.0, 1.0, RET_DK // 2))
    ang = pos.astype(F32)[:, None] * inv[None, :]
    cos = jnp.repeat(jnp.cos(ang), 2, axis=1)
    sin = jnp.sin(ang)
    sin = jnp.stack([-sin, sin], axis=-1).reshape(pos.shape[0], RET_DK)
    return cos, sin


def _retention_kernel(q_ref, k_ref, v_ref, s0_ref, dm_ref, qd_ref, kd_ref, o_ref, s_ref, st_ref,
                      *, n_sub, gl):
    c = pl.program_id(1)

    @pl.when(c == 0)
    def _():
        st_ref[...] = s0_ref[0]

    for j in range(n_sub):
        rows = slice(j * RET_CHUNK, (j + 1) * RET_CHUNK)
        for h in range(RET_HEADS):
            q = q_ref[0, rows, h * RET_DK:(h + 1) * RET_DK]
            k = k_ref[0, rows, h * RET_DK:(h + 1) * RET_DK]
            v = v_ref[0, rows, h * RET_DV:(h + 1) * RET_DV]
            st = st_ref[h]
            qk = _dot_nt(q, k) * dm_ref[h]
            q_dec = (q.astype(F32) * qd_ref[h]).astype(BF16)
            o = _dot(qk.astype(BF16), v) + _dot(q_dec, st.astype(BF16))
            k_dec = (k.astype(F32) * kd_ref[h]).astype(BF16)
            st_ref[h] = gl[h] * st + _dot_tn(k_dec, v)
            mu = jnp.mean(o, axis=-1, keepdims=True)
            oc = o - mu
            var = jnp.mean(oc * oc, axis=-1, keepdims=True)
            o_ref[0, rows, h * RET_DV:(h + 1) * RET_DV] = (oc * lax.rsqrt(var + LN_EPS)).astype(BF16)

    @pl.when(c == pl.num_programs(1) - 1)
    def _():
        s_ref[0] = st_ref[...]


def _retention_tables(length):
    lg = np.log1p(-(2.0 ** (-5.0 - np.arange(RET_HEADS, dtype=np.float64))))
    idx = np.arange(RET_CHUNK, dtype=np.float64)
    rel = idx[:, None] - idx[None, :]
    valid = (idx < length)
    dm = np.where((rel >= 0) & valid[:, None] & valid[None, :],
                  np.exp(np.maximum(rel, 0.0)[None] * lg[:, None, None]), 0.0)
    qd = np.exp((idx + 1.0)[None, :] * lg[:, None])
    kd = np.where(valid[None, :], np.exp((length - 1.0 - idx)[None, :] * lg[:, None]), 0.0)
    bc = lambda a: np.ascontiguousarray(np.broadcast_to(a[:, :, None], (RET_HEADS, RET_CHUNK, RET_DK)))
    gl = tuple(float(np.exp(length * g)) for g in lg)
    return (jnp.asarray(dm, F32), jnp.asarray(bc(qd), F32), jnp.asarray(bc(kd), F32)), gl


def _retention(rq, rk, rv, state0, length, lb):
    b, s, _ = rq.shape
    (dm, qd, kd), gl = _retention_tables(length)
    seq = lambda w: pl.BlockSpec((1, lb, w), lambda i, c: (i, c, 0))
    st_spec = pl.BlockSpec((1, RET_HEADS, RET_DK, RET_DV), lambda i, c: (i, 0, 0, 0))
    return pl.pallas_call(
        functools.partial(_retention_kernel, n_sub=lb // RET_CHUNK, gl=gl),
        out_shape=(jax.ShapeDtypeStruct((b, s, RET_V_W), BF16),
                   jax.ShapeDtypeStruct((b, RET_HEADS, RET_DK, RET_DV), F32)),
        grid=(b, s // lb),
        in_specs=[seq(RET_QK_W), seq(RET_QK_W), seq(RET_V_W), st_spec,
                  _const_spec(dm.shape), _const_spec(qd.shape), _const_spec(kd.shape)],
        out_specs=(seq(RET_V_W), st_spec),
        scratch_shapes=[pltpu.VMEM((RET_HEADS, RET_DK, RET_DV), F32)],
        compiler_params=_cparams(("parallel", "arbitrary")),
        name="retention",
    )(rq, rk, rv, state0, dm, qd, kd)


def _diff_lambda(lq1_ref, lk1_ref, lq2_ref, lk2_ref, lambda_init):
    a = jnp.sum(lq1_ref[...] * lk1_ref[...], axis=-1, keepdims=True)
    b = jnp.sum(lq2_ref[...] * lk2_ref[...], axis=-1, keepdims=True)
    return jnp.exp(a) - jnp.exp(b) + lambda_init


def _rms_head(o, nw, lambda_init):
    ms = jnp.mean(o * o, axis=-1, keepdims=True)
    return o * lax.rsqrt(ms + LN_EPS) * nw * (1.0 - lambda_init)


def _diffattn_kernel(q_ref, k_ref, v_ref, lq1_ref, lk1_ref, lq2_ref, lk2_ref, nw_ref, o_ref,
                     *, bq, lambda_init):
    qi = pl.program_id(2)
    q = q_ref[0]
    lane = lax.broadcasted_iota(jnp.int32, q.shape, 1)
    zero = jnp.zeros_like(q)
    qq = jnp.concatenate([jnp.where(lane < DIFF_DK, q, zero), jnp.where(lane >= DIFF_DK, q, zero)], axis=0)

    def step(j, carry, masked):
        m, l, acc = carry
        off = pl.multiple_of(j * bq, bq)
        kb = k_ref[0, pl.ds(off, bq), :]
        vb = v_ref[0, pl.ds(off, bq), :]
        s = _dot_nt(qq, kb)
        if masked:
            r = lax.broadcasted_iota(jnp.int32, s.shape, 0)
            r = jnp.where(r >= bq, r - bq, r)
            cidx = lax.broadcasted_iota(jnp.int32, s.shape, 1)
            s = jnp.where(cidx <= r, s, -jnp.inf)
        m_new = jnp.maximum(m, jnp.max(s, axis=-1, keepdims=True))
        p = jnp.exp(s - m_new)
        alpha = jnp.exp(m - m_new)
        l = alpha * l + jnp.sum(p, axis=-1, keepdims=True)
        acc = alpha * acc + _dot(p.astype(BF16), vb)
        return m_new, l, acc

    init = (jnp.full((2 * bq, 1), -jnp.inf, F32), jnp.zeros((2 * bq, 1), F32),
            jnp.zeros((2 * bq, DIFF_DV), F32))
    carry = lax.fori_loop(0, qi, functools.partial(step, masked=False), init)
    _, l, acc = step(qi, carry, True)
    on = acc / l
    lam = _diff_lambda(lq1_ref, lk1_ref, lq2_ref, lk2_ref, lambda_init)
    o = on[:bq] - lam * on[bq:]
    o_ref[0] = _rms_head(o, nw_ref[...], lambda_init).astype(o_ref.dtype)


def _diffattn(dq, dkb, dvb, lam_params, norm_w, lambda_init, bq):
    b, s, _ = dq.shape
    vec = _const_spec((1, DIFF_DK))
    return pl.pallas_call(
        functools.partial(_diffattn_kernel, bq=bq, lambda_init=lambda_init),
        out_shape=jax.ShapeDtypeStruct((b, s, DIFF_V_W), BF16),
        grid=(b, DIFF_HEADS, s // bq),
        in_specs=[pl.BlockSpec((1, bq, 2 * DIFF_DK), lambda i, h, j: (i, j, h)),
                  pl.BlockSpec((1, s, 2 * DIFF_DK), lambda i, h, j: (i, 0, h)),
                  pl.BlockSpec((1, s, DIFF_DV), lambda i, h, j: (i, 0, h)),
                  vec, vec, vec, vec, _const_spec((1, DIFF_DV))],
        out_specs=pl.BlockSpec((1, bq, DIFF_DV), lambda i, h, j: (i, j, h)),
        compiler_params=_cparams(("parallel", "parallel", "arbitrary")),
        name="diffattn",
    )(dq, dkb, dvb, *lam_params, norm_w)


def _decode_kernel(pt_ref, *refs, npg, n_steps, lambda_init):
    k_refs = refs[:npg]
    v_refs = refs[npg:2 * npg]
    (q_ref, kn_ref, vn_ref, lq1_ref, lk1_ref, lq2_ref, lk2_ref, nw_ref,
     o_ref, m_ref, l_ref, acc_ref) = refs[2 * npg:]
    j = pl.program_id(1)

    @pl.when(j == 0)
    def _():
        m_ref[...] = jnp.full(m_ref.shape, -jnp.inf, F32)
        l_ref[...] = jnp.zeros(l_ref.shape, F32)
        acc_ref[...] = jnp.zeros(acc_ref.shape, F32)

    q = q_ref[0]

    def update(kt, pv, mask):
        s = _dot(q, kt)
        if mask is not None:
            s = jnp.where(mask, s, -jnp.inf)
        m = m_ref[...]
        m_new = jnp.maximum(m, jnp.max(s, axis=-1, keepdims=True))
        p = jnp.exp(s - m_new)
        alpha = jnp.exp(m - m_new)
        l_ref[...] = alpha * l_ref[...] + jnp.sum(p, axis=-1, keepdims=True)
        acc_ref[...] = alpha * acc_ref[...] + pv(p.astype(BF16))
        m_ref[...] = m_new

    def pages_pv(p):
        def head_values(h):
            return jnp.concatenate(
                [v_ref[0, pl.ds(h, PAGE_SIZE, stride=DIFF_HEADS), :].astype(BF16) for v_ref in v_refs], axis=0)

        return jnp.concatenate([_dot(p, head_values(h)) for h in range(DIFF_HEADS)], axis=1)

    update(jnp.concatenate([k_ref[0].astype(BF16) for k_ref in k_refs], axis=1), pages_pv, None)

    @pl.when(j == n_steps - 1)
    def _():
        rows = 4 * 8
        r = lax.broadcasted_iota(jnp.int32, (rows, PAGE_SIZE), 0)
        cidx = lax.broadcasted_iota(jnp.int32, (rows, PAGE_SIZE), 1)
        update(kn_ref[0], lambda p: _dot(p, vn_ref[0]), cidx <= (r >> 3))
        lam = _diff_lambda(lq1_ref, lk1_ref, lq2_ref, lk2_ref, lambda_init)
        on = acc_ref[...] / l_ref[...]
        rr = lax.broadcasted_iota(jnp.int32, on.shape, 0)
        cc = lax.broadcasted_iota(jnp.int32, on.shape, 1)
        g = rr & 7
        coef = jnp.where((g & 1) == 0, 1.0, -lam)
        w = jnp.where((cc >> 7) == (g >> 1), coef, 0.0)
        o = jnp.sum((on * w).reshape(4, 8, DIFF_V_W), axis=1)
        for h in range(DIFF_HEADS):
            sl = slice(h * DIFF_DV, (h + 1) * DIFF_DV)
            o_ref[0, :, sl] = _rms_head(o[:, sl], nw_ref[...], lambda_init)


def _decode_attn(page_table, cache_kt, cache_v, qbd, kt_new, v_new, lam_params, norm_w, lambda_init):
    bd, n_pages = page_table.shape
    npg = math.gcd(n_pages, PAGES_PER_STEP)
    n_steps = n_pages // npg

    def page_spec(i):
        return pl.BlockSpec((1, DIFF_QK_W, PAGE_SIZE), lambda b, j, pt: (pt[b, j * npg + i], 0, 0))

    per_seq = lambda r, c: pl.BlockSpec((1, r, c), lambda b, j, pt: (b, 0, 0))
    vec = pl.BlockSpec((1, DIFF_DK), lambda b, j, pt: (0, 0))
    grid_spec = pltpu.PrefetchScalarGridSpec(
        num_scalar_prefetch=1,
        grid=(bd, n_steps),
        in_specs=[page_spec(i) for i in range(npg)] + [page_spec(i) for i in range(npg)]
        + [per_seq(32, DIFF_QK_W), per_seq(DIFF_QK_W, PAGE_SIZE), per_seq(PAGE_SIZE, DIFF_V_W), vec, vec, vec, vec,
           pl.BlockSpec((1, DIFF_DV), lambda b, j, pt: (0, 0))],
        out_specs=per_seq(4, DIFF_V_W),
        scratch_shapes=[pltpu.VMEM((32, 1), F32), pltpu.VMEM((32, 1), F32), pltpu.VMEM((32, DIFF_V_W), F32)],
    )
    return pl.pallas_call(
        functools.partial(_decode_kernel, npg=npg, n_steps=n_steps, lambda_init=lambda_init),
        out_shape=jax.ShapeDtypeStruct((bd, 4, DIFF_V_W), F32),
        grid_spec=grid_spec,
        compiler_params=_cparams(("parallel", "arbitrary")),
        name="decode_attn",
    )(page_table, *([cache_kt] * npg), *([cache_v] * npg), qbd, kt_new, v_new, *lam_params, norm_w)


def _layer_norm(x, w, b):
    mu = jnp.mean(x, axis=-1, keepdims=True)
    xc = x - mu
    var = jnp.mean(xc * xc, axis=-1, keepdims=True)
    return xc * lax.rsqrt(var + LN_EPS) * w + b


def _tail_kernel(x_ref, on_ref, od_ref, wg_ref, rnw_ref, wur_ref, wud_ref, wo_ref, l1w_ref, l1b_ref,
                 sw1_ref, sw3_ref, sw2_ref, h_ref, base_ref, *, alpha):
    x = x_ref[...]
    xb = x.astype(BF16)
    rg = _dot(xb, wg_ref[:, 0:1024])
    o_ret = _silu(rg) * (on_ref[...].astype(F32) * rnw_ref[...])
    u = _sigmoid(_dot(xb, wg_ref[:, 1024:2048])) * _dot(o_ret.astype(BF16), wur_ref[...])
    u = u + _sigmoid(_dot(xb, wg_ref[:, 2048:3072])) * _dot(od_ref[...], wud_ref[...])
    h = _layer_norm(alpha * x + _dot(u.astype(BF16), wo_ref[...]), l1w_ref[...], l1b_ref[...])
    hb = h.astype(BF16)
    a = _silu(_dot(hb, sw1_ref[...])) * _dot(hb, sw3_ref[...])
    _store_slabs(h_ref, _pack_row(h))
    base_ref[...] = alpha * h + _dot(a.astype(BF16), sw2_ref[...])


def _tail(x2d, on, od, w, alpha, tm):
    t = x2d.shape[0]
    row = lambda wd: pl.BlockSpec((tm, wd), lambda i: (i, 0))
    names = ("wg", "ret_norm_w", "w_up_ret", "w_up_diff", "w_out", "ln1_w", "ln1_b", "sw1", "sw3", "sw2")
    ws = [w[n] for n in names]
    return pl.pallas_call(
        functools.partial(_tail_kernel, alpha=alpha),
        out_shape=(jax.ShapeDtypeStruct((t * ROW_SLABS, LANES), jnp.uint32),
                   jax.ShapeDtypeStruct((t, D_MODEL), F32)),
        grid=(t // tm,),
        in_specs=[row(D_MODEL), row(RET_V_W), row(DIFF_V_W)] + [_const_spec(a.shape) for a in ws],
        out_specs=(pl.BlockSpec((tm * ROW_SLABS, LANES), lambda i: (i, 0)), row(D_MODEL)),
        compiler_params=_cparams(("parallel",)),
        name="tail",
    )(x2d, on, od, *ws)


def _first_argmax(x, iota, size):
    m = jnp.max(x, axis=0, keepdims=True)
    idx = jnp.min(jnp.where(x == m, iota, size), axis=0, keepdims=True)
    return m, idx


def _router_kernel(h_ref, rw_ref, rb_ref, eidx_ref, gate_ref, pos_ref, cnt_ref, carry_ref):
    i = pl.program_id(0)
    tb = h_ref.shape[0] // ROW_SLABS
    per = N_EXPERTS // N_GROUPS

    @pl.when(i == 0)
    def _():
        carry_ref[...] = jnp.zeros(carry_ref.shape, F32)

    s = _sigmoid(_dot_nt(rw_ref[...], _unpack_row(_load_slabs(h_ref, tb)).astype(BF16)))
    sb = s + rb_ref[...]
    neg = -jnp.inf

    sb3 = sb.reshape(N_GROUPS, per, tb)
    io3 = lax.broadcasted_iota(jnp.int32, sb3.shape, 1)
    m1 = jnp.max(sb3, axis=1, keepdims=True)
    i1 = jnp.min(jnp.where(sb3 == m1, io3, per), axis=1, keepdims=True)
    m2 = jnp.max(jnp.where(io3 == i1, neg, sb3), axis=1, keepdims=True)
    gscore = (m1 + m2).reshape(N_GROUPS, tb)

    iog = lax.broadcasted_iota(jnp.int32, gscore.shape, 0)
    gsel = jnp.zeros(gscore.shape, F32)
    for _ in range(TOPK_GROUPS):
        _, gi = _first_argmax(gscore, iog, N_GROUPS)
        hit = iog == gi
        gsel = jnp.where(hit, 1.0, gsel)
        gscore = jnp.where(hit, neg, gscore)

    emask = jnp.broadcast_to(gsel.reshape(N_GROUPS, 1, tb), (N_GROUPS, per, tb)).reshape(N_EXPERTS, tb)
    cand = jnp.where(emask > 0.0, sb, neg)
    ioe = lax.broadcasted_iota(jnp.int32, cand.shape, 0)
    chosen = jnp.zeros(cand.shape, F32)
    idxs, gates = [], []
    for _ in range(TOP_K):
        _, ei = _first_argmax(cand, ioe, N_EXPERTS)
        hit = ioe == ei
        gates.append(jnp.sum(jnp.where(hit, s, 0.0), axis=0, keepdims=True))
        idxs.append(ei)
        chosen = jnp.where(hit, 1.0, chosen)
        cand = jnp.where(hit, neg, cand)

    gsum = gates[0]
    for g in gates[1:]:
        gsum = gsum + g
    scale = ROUTED_SCALE / gsum
    for k in range(TOP_K):
        eidx_ref[k:k + 1, :] = idxs[k]
        gate_ref[k:k + 1, :] = gates[k] * scale

    r = lax.broadcasted_iota(jnp.int32, (tb, tb), 0)
    cidx = lax.broadcasted_iota(jnp.int32, (tb, tb), 1)
    before = jnp.where(r < cidx, 1.0, 0.0).astype(BF16)
    rank = carry_ref[...] + _dot(chosen.astype(BF16), before)
    for k in range(TOP_K):
        pk = jnp.sum(jnp.where(ioe == idxs[k], rank, 0.0), axis=0, keepdims=True)
        pos_ref[k:k + 1, :] = pk.astype(jnp.int32)
    total = carry_ref[...] + jnp.sum(chosen, axis=1, keepdims=True)
    carry_ref[...] = total
    cnt_ref[...] = total.astype(jnp.int32)


def _router(h, rw_t, rb_col, tb):
    t = h.shape[0] // ROW_SLABS
    slot = pl.BlockSpec((TOP_K, tb), lambda i: (0, i))
    return pl.pallas_call(
        _router_kernel,
        out_shape=(jax.ShapeDtypeStruct((TOP_K, t), jnp.int32), jax.ShapeDtypeStruct((TOP_K, t), F32),
                   jax.ShapeDtypeStruct((TOP_K, t), jnp.int32), jax.ShapeDtypeStruct((N_EXPERTS, 1), jnp.int32)),
        grid=(t // tb,),
        in_specs=[pl.BlockSpec((tb * ROW_SLABS, LANES), lambda i: (i, 0)), _const_spec(rw_t.shape),
                  _const_spec(rb_col.shape)],
        out_specs=(slot, slot, slot, _const_spec((N_EXPERTS, 1))),
        scratch_shapes=[pltpu.VMEM((N_EXPERTS, 1), F32)],
        compiler_params=_cparams(("arbitrary",)),
        name="router",
    )(h, rw_t, rb_col)


def _row_copy_wait(src_rows, dst_rows, sem):
    pltpu.make_async_copy(src_rows, dst_rows, sem).wait()


_PAD_CHUNKS = tuple(1 << s for s in reversed(range(MOE_BLOCK.bit_length() - 1)))


def _zero_fill_padding(ps_ref, cnt_ref, zero_ref, xs_ref, sem):
    def chunks(e, act):
        cnt = cnt_ref[e]
        n_pad = (MOE_BLOCK - (cnt & (MOE_BLOCK - 1))) & (MOE_BLOCK - 1)
        off = ps_ref[e] + cnt
        for rows in _PAD_CHUNKS:
            take = n_pad & rows

            @pl.when(take != 0)
            def _(off=off, rows=rows):
                act(pltpu.make_async_copy(zero_ref.at[pl.ds(0, rows)], xs_ref.at[pl.ds(off, rows)], sem))

            off = off + take

    def start(e, carry):
        chunks(e, lambda cp: cp.start())
        return carry

    def wait(e, carry):
        chunks(e, lambda cp: cp.wait())
        return carry

    lax.fori_loop(0, N_EXPERTS, start, 0)
    lax.fori_loop(0, N_EXPERTS, wait, 0)


def _zero_fill_unused_blocks(nb_ref, zero_ref, xs_ref, sem):
    rows = zero_ref.shape[0]
    n_total = xs_ref.shape[0] // MOE_BLOCK

    def copies(blk, act):
        for j in range(MOE_BLOCK // rows):
            act(pltpu.make_async_copy(zero_ref, xs_ref.at[pl.ds(blk * MOE_BLOCK + j * rows, rows)], sem))

    def start(blk, carry):
        copies(blk, lambda cp: cp.start())
        return carry

    def wait(blk, carry):
        copies(blk, lambda cp: cp.wait())
        return carry

    lax.fori_loop(nb_ref[0], n_total, start, 0)
    lax.fori_loop(nb_ref[0], n_total, wait, 0)


def _dest_kernel(ps_ref, eidx_ref, pos_ref, dest_ref):
    ps = ps_ref[...]
    ioe = lax.broadcasted_iota(jnp.int32, (N_EXPERTS, eidx_ref.shape[1]), 0)
    for k in range(TOP_K):
        start = jnp.sum(jnp.where(ioe == eidx_ref[k:k + 1, :], ps, 0.0), axis=0, keepdims=True)
        dest_ref[k:k + 1, :] = start.astype(jnp.int32) + pos_ref[k:k + 1, :]


def _dest_rows(pad_start, eidx, pos, tb):
    t = eidx.shape[1]
    slot = pl.BlockSpec((TOP_K, tb), lambda i: (0, i))
    return pl.pallas_call(
        _dest_kernel,
        out_shape=jax.ShapeDtypeStruct((TOP_K, t), jnp.int32),
        grid=(t // tb,),
        in_specs=[_const_spec((N_EXPERTS, 1)), slot, slot],
        out_specs=slot,
        compiler_params=_cparams(("parallel",)),
        name="dest_rows",
    )(pad_start.astype(F32).reshape(N_EXPERTS, 1), eidx, pos)


def _dispatch_kernel(ps_ref, cnt_ref, nb_ref, dest_ref, h_ref, xs_ref, zero_ref, sem, zsem):
    per_tile = SUBLANES // ROW_SLABS
    tm = h_ref.shape[0] // ROW_SLABS

    @pl.when(pl.program_id(0) == 0)
    def _():
        zero_ref[...] = jnp.zeros(zero_ref.shape, zero_ref.dtype)
        _zero_fill_padding(ps_ref, cnt_ref, zero_ref, xs_ref, zsem)
        _zero_fill_unused_blocks(nb_ref, zero_ref, xs_ref, zsem)

    def body(u, carry):
        for par in range(per_tile):
            t = u * per_tile + par
            for k in range(TOP_K):
                pltpu.make_async_copy(_slab(h_ref, u, par), xs_ref.at[dest_ref[k, t]], sem).start(priority=k % 2)
        return carry

    lax.fori_loop(0, tm // per_tile, body, 0)
    for k in range(TOP_K):
        _row_copy_wait(h_ref, h_ref, sem)


def _dispatch(pad_start, counts, n_used, dest, h, n_rows, tm):
    t = h.shape[0] // ROW_SLABS
    slot = pl.BlockSpec((TOP_K, tm), lambda i, ps, cnt, nb: (0, i), memory_space=pltpu.SMEM)
    grid_spec = pltpu.PrefetchScalarGridSpec(
        num_scalar_prefetch=3,
        grid=(t // tm,),
        in_specs=[slot, pl.BlockSpec((tm * ROW_SLABS, LANES), lambda i, ps, cnt, nb: (i, 0))],
        out_specs=pl.BlockSpec(memory_space=pl.ANY),
        scratch_shapes=[pltpu.VMEM((_PAD_CHUNKS[0], ROW_SLABS, LANES), jnp.uint32), pltpu.SemaphoreType.DMA(()),
                        pltpu.SemaphoreType.DMA(())],
    )
    return pl.pallas_call(
        _dispatch_kernel,
        out_shape=jax.ShapeDtypeStruct((n_rows, ROW_SLABS, LANES), jnp.uint32),
        grid_spec=grid_spec,
        compiler_params=_cparams(("arbitrary",)),
        name="dispatch",
    )(pad_start, counts, n_used, dest, h)


def _expert_kernel(be_ref, nb_ref, xs_ref, w1_ref, w3_ref, w2_ref, ys_ref, w1b_ref, w3b_ref, w2b_ref):
    i = pl.program_id(0)

    @pl.when(jnp.logical_or(i == 0, be_ref[i] != be_ref[jnp.maximum(i - 1, 0)]))
    def _():
        w1b_ref[...] = w1_ref[0].astype(BF16)
        w3b_ref[...] = w3_ref[0].astype(BF16)
        w2b_ref[...] = w2_ref[0].astype(BF16)

    @pl.when(i < nb_ref[0])
    def _():
        xb = _unpack_row(_load_slabs(xs_ref, MOE_BLOCK)).astype(BF16)
        a = _silu(_dot(xb, w1b_ref[...])) * _dot(xb, w3b_ref[...])
        _store_slabs(ys_ref, _pack_row(_dot(a.astype(BF16), w2b_ref[...])))

    @pl.when(i >= nb_ref[0])
    def _():
        ys_ref[...] = jnp.zeros(ys_ref.shape, ys_ref.dtype)


def _experts(block_expert, n_used, xs, w1, w3, w2):
    n_rows = xs.shape[0] // ROW_SLABS
    grid_spec = pltpu.PrefetchScalarGridSpec(
        num_scalar_prefetch=2,
        grid=(n_rows // MOE_BLOCK,),
        in_specs=[pl.BlockSpec((MOE_BLOCK * ROW_SLABS, LANES), lambda i, be, nb: (jnp.minimum(i, nb[0] - 1), 0)),
                  pl.BlockSpec((1, D_MODEL, D_EXPERT), lambda i, be, nb: (be[i], 0, 0)),
                  pl.BlockSpec((1, D_MODEL, D_EXPERT), lambda i, be, nb: (be[i], 0, 0)),
                  pl.BlockSpec((1, D_EXPERT, D_MODEL), lambda i, be, nb: (be[i], 0, 0))],
        out_specs=pl.BlockSpec((MOE_BLOCK * ROW_SLABS, LANES), lambda i, be, nb: (i, 0)),
        scratch_shapes=[pltpu.VMEM((D_MODEL, D_EXPERT), BF16), pltpu.VMEM((D_MODEL, D_EXPERT), BF16),
                        pltpu.VMEM((D_EXPERT, D_MODEL), BF16)],
    )
    return pl.pallas_call(
        _expert_kernel,
        out_shape=jax.ShapeDtypeStruct((n_rows * ROW_SLABS, LANES), jnp.uint32),
        grid_spec=grid_spec,
        compiler_params=_cparams(("arbitrary",)),
        name="experts",
    )(block_expert, n_used, xs, w1, w3, w2)


def _combine_kernel(dcur_ref, dnxt_ref, base_ref, gate_ref, p_ref, ys_ref,
                    l2w_ref, l2b_ref, wpg_ref, wpe_ref, y_ref, rows_ref, sems):
    i = pl.program_id(0)
    n = pl.num_programs(0)
    tm = base_ref.shape[0]

    per_tile = SUBLANES // ROW_SLABS

    def start_gather(d_ref, slot):
        def body(u, carry):
            for par in range(per_tile):
                t = u * per_tile + par
                for k in range(TOP_K):
                    pltpu.make_async_copy(ys_ref.at[d_ref[k, t]], _slab(rows_ref.at[slot, k], u, par),
                                          sems.at[slot]).start(priority=k % 2)
            return carry

        lax.fori_loop(0, tm // per_tile, body, 0)

    @pl.when(i == 0)
    def _():
        start_gather(dcur_ref, 0)

    slot = i % 2

    def for_this_slot(fn):
        for s in range(2):
            @pl.when(slot == s)
            def _(s=s):
                fn(s)

    @pl.when(i + 1 < n)
    def _():
        for_this_slot(lambda s: start_gather(dnxt_ref, 1 - s))

    def wait_rows(s):
        for k in range(TOP_K):
            _row_copy_wait(rows_ref.at[s, k], rows_ref.at[s, k], sems.at[s])

    for_this_slot(wait_rows)

    g = gate_ref[...]
    m = base_ref[...]
    for k in range(TOP_K):
        m = m + _unpack_row(_load_slabs(rows_ref.at[slot, k], tm)) * g[:, k:k + 1]
    h2 = _layer_norm(m, l2w_ref[...], l2b_ref[...])
    emb = _dot(p_ref[...].astype(BF16), wpe_ref[...])
    y_ref[...] = h2 + _sigmoid(_dot(h2.astype(BF16), wpg_ref[...])) * emb


def _combine(dest, base, gate_t, p2d, ys, w, tm):
    t = base.shape[0]
    n = t // tm
    row = lambda wd: pl.BlockSpec((tm, wd), lambda i: (i, 0))
    cur = pl.BlockSpec((TOP_K, tm), lambda i: (0, i), memory_space=pltpu.SMEM)
    nxt = pl.BlockSpec((TOP_K, tm), lambda i: (0, jnp.minimum(i + 1, n - 1)), memory_space=pltpu.SMEM)
    ws = [w[nm] for nm in ("ln2_w", "ln2_b", "w_pg", "w_pe")]
    return pl.pallas_call(
        _combine_kernel,
        out_shape=jax.ShapeDtypeStruct((t, D_MODEL), F32),
        grid=(n,),
        in_specs=[cur, nxt, row(D_MODEL), row(TOP_K), row(p2d.shape[1]), pl.BlockSpec(memory_space=pl.ANY)]
        + [_const_spec(a.shape) for a in ws],
        out_specs=row(D_MODEL),
        scratch_shapes=[pltpu.VMEM((2, TOP_K, tm * ROW_SLABS, LANES), jnp.uint32), pltpu.SemaphoreType.DMA((2,))],
        compiler_params=_cparams(("arbitrary",)),
        name="combine",
    )(dest, dest, base, gate_t, p2d, ys, *ws)


def _moe_and_out(h, base, p2d, w, tb, tm_disp, tm_comb):
    t = h.shape[0] // ROW_SLABS
    eidx, gate, pos, counts = _router(h, w["router_wt"], w["router_b"], tb)
    counts = counts[:, 0]
    padded = (counts + MOE_BLOCK - 1) // MOE_BLOCK * MOE_BLOCK
    pad_end = jnp.cumsum(padded)
    pad_start = pad_end - padded
    n_blocks = (t * TOP_K + N_EXPERTS * (MOE_BLOCK - 1) + MOE_BLOCK - 1) // MOE_BLOCK
    block_expert = jnp.minimum(
        jnp.searchsorted(pad_end, jnp.arange(n_blocks, dtype=jnp.int32) * MOE_BLOCK, side="right"),
        N_EXPERTS - 1).astype(jnp.int32)
    n_used = (pad_end[-1:] // MOE_BLOCK).astype(jnp.int32)
    dest = _dest_rows(pad_start, eidx, pos, tb)
    xs = _dispatch(pad_start, counts, n_used, dest, h, n_blocks * MOE_BLOCK, tm_disp)
    ys = _experts(block_expert, n_used, xs.reshape(-1, LANES), w["ew1"], w["ew3"], w["ew2"])
    return _combine(dest, base, gate.T, p2d, ys.reshape(-1, ROW_SLABS, LANES), w, tm_comb)


def _layer_weights(i, w_in, ret_norm_w, diff_norm_w, w_up_ret, w_up_diff, w_out, ln1_w, ln1_b, router_w, router_b,
                   expert_w1, expert_w3, expert_w2, shared_w1, shared_w3, shared_w2, ln2_w, ln2_b, w_pe, w_pg):
    o = _OFFS
    wi = w_in[i]
    vec = lambda a: a[i].reshape(1, -1).astype(F32)
    return {
        "w_mix": jnp.concatenate([wi[:, o[0]:o[3]], wi[:, o[4]:o[7]]], axis=1).astype(BF16),
        "wg": jnp.concatenate([wi[:, o[3]:o[4]], wi[:, o[7]:o[9]]], axis=1).astype(BF16),
        "ret_norm_w": vec(ret_norm_w), "diff_norm_w": vec(diff_norm_w),
        "w_up_ret": w_up_ret[i].astype(BF16), "w_up_diff": w_up_diff[i].astype(BF16), "w_out": w_out[i].astype(BF16),
        "ln1_w": vec(ln1_w), "ln1_b": vec(ln1_b), "ln2_w": vec(ln2_w), "ln2_b": vec(ln2_b),
        "router_wt": router_w[i].T.astype(BF16), "router_b": router_b[i].reshape(-1, 1).astype(F32),
        "ew1": expert_w1[i], "ew3": expert_w3[i], "ew2": expert_w2[i],
        "sw1": shared_w1[i].astype(BF16), "sw3": shared_w3[i].astype(BF16), "sw2": shared_w2[i].astype(BF16),
        "w_pe": w_pe[i].astype(BF16), "w_pg": w_pg[i].astype(BF16),
    }


def _pick(n, pref):
    return pref if n % pref == 0 else n


def kernel(x_prompt, x_sample, cache_diff_k, cache_diff_v, state_retention, page_table, p_prompt, p_sample, w_in, ret_norm_w, diff_lq1, diff_lk1, diff_lq2, diff_lk2, diff_norm_w, w_up_ret, w_up_diff, w_out, ln1_w, ln1_b, router_w, router_b, expert_w1, expert_w3, expert_w2, shared_w1, shared_w3, shared_w2, ln2_w, ln2_b, w_pe, w_pg):
    depth = w_in.shape[0]
    b, s, d = x_prompt.shape
    bd, ls, _ = x_sample.shape
    n_pages = page_table.shape[1]
    past = n_pages * PAGE_SIZE
    alpha = (2 * depth) ** 0.25
    tp, ts = b * s, bd * ls

    cos_p, sin_p = _rotation_tables(jnp.arange(s))
    cos_s, sin_s = _rotation_tables(past + jnp.arange(ls))
    cos_s = jnp.tile(cos_s, (bd, 1))
    sin_s = jnp.tile(sin_s, (bd, 1))

    yp, ys = x_prompt.reshape(tp, d), x_sample.reshape(ts, d)
    kp_l, vp_l, sp_l, ks_l, vs_l, ss_l = [], [], [], [], [], []
    for i in range(depth):
        lambda_init = 0.8 - 0.6 * math.exp(-0.3 * i)
        w = _layer_weights(i, w_in, ret_norm_w, diff_norm_w, w_up_ret, w_up_diff, w_out, ln1_w, ln1_b, router_w,
                           router_b, expert_w1, expert_w3, expert_w2, shared_w1, shared_w3, shared_w2,
                           ln2_w, ln2_b, w_pe, w_pg)
        lam_params = tuple(a[i].reshape(1, DIFF_DK).astype(F32) for a in (diff_lq1, diff_lk1, diff_lq2, diff_lk2))

        rq, rk, rv, dq, dk, dv, dkb, dvb = _proj(yp, w["w_mix"], cos_p, sin_p, _pick(s, 512))
        seq = lambda a: a.reshape(b, s, a.shape[-1])
        on, st_p = _retention(seq(rq), seq(rk), seq(rv), jnp.zeros((b, RET_HEADS, RET_DK, RET_DV), F32),
                              RET_CHUNK, _pick(s, 512))
        od = _diffattn(seq(dq), seq(dkb), seq(dvb), lam_params, w["diff_norm_w"], lambda_init, _pick(s, 512))
        h, base = _tail(yp, on.reshape(tp, RET_V_W), od.reshape(tp, DIFF_V_W), w, alpha, _pick(tp, 256))
        yp = _moe_and_out(h, base, p_prompt[i].reshape(tp, -1), w, _pick(tp, 512), _pick(tp, 256), _pick(tp, 128))
        kp_l.append(dk.reshape(b, s, DIFF_HEADS, 2, DIFF_DK))
        vp_l.append(dv.reshape(b, s, DIFF_HEADS, DIFF_DV))
        sp_l.append(st_p)

        rq, rk, rv, dq, dk, dv, dkb, dvb = _proj(ys, w["w_mix"], cos_s, sin_s, _pick(ts, 512))
        pad_rows = lambda a, r: jnp.pad(a.reshape(bd, ls, a.shape[-1]), ((0, 0), (0, r - ls), (0, 0)))
        on, st_s = _retention(pad_rows(rq, RET_CHUNK), pad_rows(rk, RET_CHUNK), pad_rows(rv, RET_CHUNK),
                              state_retention[i], ls, RET_CHUNK)
        on = on[:, :ls].reshape(ts, RET_V_W)
        q5 = dq.reshape(bd, ls, 1, 2 * DIFF_HEADS, DIFF_DK)
        eye = jnp.eye(2 * DIFF_HEADS, dtype=BF16)[None, None, :, :, None]
        qbd = (q5 * eye).reshape(bd, ls * 2 * DIFF_HEADS, DIFF_QK_W)
        cache_kt = jnp.transpose(cache_diff_k[i], (0, 2, 3, 4, 1)).reshape(-1, DIFF_QK_W, PAGE_SIZE)
        cache_v = cache_diff_v[i].reshape(-1, PAGE_SIZE * DIFF_HEADS, DIFF_DV)
        kt_new = jnp.pad(jnp.swapaxes(dkb.reshape(bd, ls, DIFF_QK_W), 1, 2), ((0, 0), (0, 0), (0, PAGE_SIZE - ls)))
        od = _decode_attn(page_table, cache_kt, cache_v, qbd, kt_new, pad_rows(dvb, PAGE_SIZE), lam_params,
                          w["diff_norm_w"], lambda_init)
        od = od.reshape(ts, DIFF_V_W).astype(BF16)
        h, base = _tail(ys, on, od, w, alpha, _pick(ts, 256))
        ys = _moe_and_out(h, base, p_sample[i].reshape(ts, -1), w, _pick(ts, 512), _pick(ts, 256), _pick(ts, 128))
        ks_l.append(dk.reshape(bd, ls, DIFF_HEADS, 2, DIFF_DK))
        vs_l.append(dv.reshape(bd, ls, DIFF_HEADS, DIFF_DV))
        ss_l.append(st_s)

    return (yp.reshape(b, s, d), ys.reshape(bd, ls, d), jnp.stack(kp_l), jnp.stack(vp_l), jnp.stack(sp_l),
            jnp.stack(ks_l), jnp.stack(vs_l), jnp.stack(ss_l))
```

```python
import functools
import math

import jax
import jax.numpy as jnp
import numpy as np
from jax import lax
from jax.experimental import pallas as pl
from jax.experimental.pallas import tpu as pltpu

F32 = jnp.float32
BF16 = jnp.bfloat16

D_MODEL = 1024
PAGE_SIZE = 128
RET_HEADS = 4
RET_DK = 128
RET_DV = 256
RET_CHUNK = 128
DIFF_HEADS = 4
DIFF_DK = 64
DIFF_DV = 128
N_EXPERTS = 256
TOP_K = 8
N_GROUPS = 8
TOPK_GROUPS = 4
D_EXPERT = 256
ROUTED_SCALE = 2.5
LN_EPS = 1e-5

RET_QK_W = RET_HEADS * RET_DK
RET_V_W = RET_HEADS * RET_DV
DIFF_QK_W = DIFF_HEADS * 2 * DIFF_DK
DIFF_V_W = DIFF_HEADS * DIFF_DV
_SPLITS = (RET_QK_W, RET_QK_W, RET_V_W, RET_V_W, DIFF_QK_W, DIFF_QK_W, DIFF_V_W, D_MODEL, D_MODEL)
_OFFS = tuple(int(o) for o in np.cumsum((0,) + _SPLITS))

LANES = 128
SUBLANES = 8
VMEM_LIMIT = 56 * 1024 * 1024

MOE_BLOCK = 256
PAGES_PER_STEP = 16


def _cparams(sem, vmem=VMEM_LIMIT):
    return pltpu.CompilerParams(dimension_semantics=sem, vmem_limit_bytes=vmem)


def _const_spec(shape):
    nd = len(shape)
    return pl.BlockSpec(shape, lambda *_: (0,) * nd)


def _sigmoid(x):
    return 1.0 / (1.0 + jnp.exp(-x))


def _silu(x):
    return x * _sigmoid(x)


PACK_W = D_MODEL // 2
ROW_SLABS = PACK_W // LANES
_HI_MASK = np.uint32(0xFFFF0000)


def _pack_row(x):
    bits = lambda v: lax.bitcast_convert_type(v.astype(BF16).astype(F32), jnp.uint32)
    return (bits(x[:, :PACK_W]) >> 16) | (bits(x[:, PACK_W:]) & _HI_MASK)


def _unpack_row(p):
    lo = lax.bitcast_convert_type(p << 16, F32)
    hi = lax.bitcast_convert_type(p & _HI_MASK, F32)
    return jnp.concatenate([lo, hi], axis=1)


def _store_slabs(ref, packed):
    rows = packed.shape[0]
    for j in range(ROW_SLABS):
        ref[pl.ds(j, rows, stride=ROW_SLABS), :] = packed[:, j * LANES:(j + 1) * LANES]


def _load_slabs(ref, rows):
    return jnp.concatenate([ref[pl.ds(j, rows, stride=ROW_SLABS), :] for j in range(ROW_SLABS)], axis=1)


def _slab(ref, t_pair, par):
    tile = ref.at[pl.ds(pl.multiple_of(t_pair * SUBLANES, SUBLANES), SUBLANES)]
    return tile.at[par * ROW_SLABS:(par + 1) * ROW_SLABS]


def _dot(a, b):
    return jnp.dot(a, b, preferred_element_type=F32)


def _dot_nt(a, b):
    return lax.dot_general(a, b, (((1,), (1,)), ((), ())), preferred_element_type=F32)


def _dot_tn(a, b):
    return lax.dot_general(a, b, (((0,), (0,)), ((), ())), preferred_element_type=F32)


def _swap_pairs(x):
    lane = lax.broadcasted_iota(jnp.int32, x.shape, 1)
    nxt = pltpu.roll(x, LANES - 1, 1)
    prv = pltpu.roll(x, 1, 1)
    return jnp.where((lane & 1) == 0, nxt, prv)


def _proj_kernel(x_ref, w_ref, cos_ref, sin_ref,
                 rq_ref, rk_ref, rv_ref, dq_ref, dk_ref, dv_ref, dkb_ref, dvb_ref):
    xb = x_ref[...].astype(BF16)
    cos = cos_ref[...]
    sin = sin_ref[...]

    def mm(lo, hi):
        return _dot(xb, w_ref[:, lo:hi])

    q = mm(0, 512)
    k = mm(512, 1024)
    for h in range(RET_HEADS):
        sl = slice(h * RET_DK, (h + 1) * RET_DK)
        qh = q[:, sl]
        kh = k[:, sl]
        rq_ref[:, sl] = (qh * cos + _swap_pairs(qh) * sin).astype(BF16)
        rk_ref[:, sl] = ((kh * cos + _swap_pairs(kh) * sin) * (RET_DK ** -0.5)).astype(BF16)
    rv_ref[...] = mm(1024, 2048).astype(BF16)
    dq_ref[...] = (mm(2048, 2560) * (DIFF_DK ** -0.5)).astype(BF16)
    tm = x_ref.shape[0]
    dk = mm(2560, 3072)
    for g in range(2 * DIFF_HEADS):
        dk_ref[pl.ds(g, tm, stride=2 * DIFF_HEADS), :] = dk[:, g * DIFF_DK:(g + 1) * DIFF_DK]
    dkb_ref[...] = dk.astype(BF16)
    dv = mm(3072, 3584)
    for h in range(DIFF_HEADS):
        dv_ref[pl.ds(h, tm, stride=DIFF_HEADS), :] = dv[:, h * DIFF_DV:(h + 1) * DIFF_DV]
    dvb_ref[...] = dv.astype(BF16)


def _proj(x2d, w_mix, cos_t, sin_t, tm):
    t = x2d.shape[0]
    nt = cos_t.shape[0] // tm
    row = lambda w: pl.BlockSpec((tm, w), lambda i: (i, 0))
    tab = pl.BlockSpec((tm, LANES), lambda i: (i % nt, 0))
    out_shapes = (
        jax.ShapeDtypeStruct((t, RET_QK_W), BF16), jax.ShapeDtypeStruct((t, RET_QK_W), BF16),
        jax.ShapeDtypeStruct((t, RET_V_W), BF16), jax.ShapeDtypeStruct((t, DIFF_QK_W), BF16),
        jax.ShapeDtypeStruct((t * 2 * DIFF_HEADS, DIFF_DK), F32), jax.ShapeDtypeStruct((t * DIFF_HEADS, DIFF_DV), F32),
        jax.ShapeDtypeStruct((t, DIFF_QK_W), BF16), jax.ShapeDtypeStruct((t, DIFF_V_W), BF16),
    )
    k_rows = pl.BlockSpec((tm * 2 * DIFF_HEADS, DIFF_DK), lambda i: (i, 0))
    v_rows = pl.BlockSpec((tm * DIFF_HEADS, DIFF_DV), lambda i: (i, 0))
    return pl.pallas_call(
        _proj_kernel,
        out_shape=out_shapes,
        grid=(t // tm,),
        in_specs=[row(D_MODEL), _const_spec(w_mix.shape), tab, tab],
        out_specs=(row(512), row(512), row(1024), row(512), k_rows, v_rows, row(512), row(512)),
        compiler_params=_cparams(("parallel",)),
        name="proj",
    )(x2d, w_mix, cos_t, sin_t)


def _rotation_tables(pos):
    inv = 1.0 / (10000.0 ** jnp.linspace(0.0, 1.0, RET_DK // 2))
    ang = pos.astype(F32)[:, None] * inv[None, :]
    cos = jnp.repeat(jnp.cos(ang), 2, axis=1)
    sin = jnp.sin(ang)
    sin = jnp.stack([-sin, sin], axis=-1).reshape(pos.shape[0], RET_DK)
    return cos, sin


def _retention_kernel(q_ref, k_ref, v_ref, s0_ref, dm_ref, qd_ref, kd_ref, o_ref, s_ref, st_ref,
                      *, n_sub, gl):
    c = pl.program_id(1)

    @pl.when(c == 0)
    def _():
        st_ref[...] = s0_ref[0]

    for j in range(n_sub):
        rows = slice(j * RET_CHUNK, (j + 1) * RET_CHUNK)
        for h in range(RET_HEADS):
            q = q_ref[0, rows, h * RET_DK:(h + 1) * RET_DK]
            k = k_ref[0, rows, h * RET_DK:(h + 1) * RET_DK]
            v = v_ref[0, rows, h * RET_DV:(h + 1) * RET_DV]
            st = st_ref[h]
            qk = _dot_nt(q, k) * dm_ref[h]
            q_dec = (q.astype(F32) * qd_ref[h]).astype(BF16)
            o = _dot(qk.astype(BF16), v) + _dot(q_dec, st.astype(BF16))
            k_dec = (k.astype(F32) * kd_ref[h]).astype(BF16)
            st_ref[h] = gl[h] * st + _dot_tn(k_dec, v)
            mu = jnp.mean(o, axis=-1, keepdims=True)
            oc = o - mu
            var = jnp.mean(oc * oc, axis=-1, keepdims=True)
            o_ref[0, rows, h * RET_DV:(h + 1) * RET_DV] = (oc * lax.rsqrt(var + LN_EPS)).astype(BF16)

    @pl.when(c == pl.num_programs(1) - 1)
    def _():
        s_ref[0] = st_ref[...]


def _retention_tables(length):
    lg = np.log1p(-(2.0 ** (-5.0 - np.arange(RET_HEADS, dtype=np.float64))))
    idx = np.arange(RET_CHUNK, dtype=np.float64)
    rel = idx[:, None] - idx[None, :]
    valid = (idx < length)
    dm = np.where((rel >= 0) & valid[:, None] & valid[None, :],
                  np.exp(np.maximum(rel, 0.0)[None] * lg[:, None, None]), 0.0)
    qd = np.exp((idx + 1.0)[None, :] * lg[:, None])
    kd = np.where(valid[None, :], np.exp((length - 1.0 - idx)[None, :] * lg[:, None]), 0.0)
    bc = lambda a: np.ascontiguousarray(np.broadcast_to(a[:, :, None], (RET_HEADS, RET_CHUNK, RET_DK)))
    gl = tuple(float(np.exp(length * g)) for g in lg)
    return (jnp.asarray(dm, F32), jnp.asarray(bc(qd), F32), jnp.asarray(bc(kd), F32)), gl


def _retention(rq, rk, rv, state0, length, lb):
    b, s, _ = rq.shape
    (dm, qd, kd), gl = _retention_tables(length)
    seq = lambda w: pl.BlockSpec((1, lb, w), lambda i, c: (i, c, 0))
    st_spec = pl.BlockSpec((1, RET_HEADS, RET_DK, RET_DV), lambda i, c: (i, 0, 0, 0))
    return pl.pallas_call(
        functools.partial(_retention_kernel, n_sub=lb // RET_CHUNK, gl=gl),
        out_shape=(jax.ShapeDtypeStruct((b, s, RET_V_W), BF16),
                   jax.ShapeDtypeStruct((b, RET_HEADS, RET_DK, RET_DV), F32)),
        grid=(b, s // lb),
        in_specs=[seq(RET_QK_W), seq(RET_QK_W), seq(RET_V_W), st_spec,
                  _const_spec(dm.shape), _const_spec(qd.shape), _const_spec(kd.shape)],
        out_specs=(seq(RET_V_W), st_spec),
        scratch_shapes=[pltpu.VMEM((RET_HEADS, RET_DK, RET_DV), F32)],
        compiler_params=_cparams(("parallel", "arbitrary")),
        name="retention",
    )(rq, rk, rv, state0, dm, qd, kd)


def _diff_lambda(lq1_ref, lk1_ref, lq2_ref, lk2_ref, lambda_init):
    a = jnp.sum(lq1_ref[...] * lk1_ref[...], axis=-1, keepdims=True)
    b = jnp.sum(lq2_ref[...] * lk2_ref[...], axis=-1, keepdims=True)
    return jnp.exp(a) - jnp.exp(b) + lambda_init


def _rms_head(o, nw, lambda_init):
    ms = jnp.mean(o * o, axis=-1, keepdims=True)
    return o * lax.rsqrt(ms + LN_EPS) * nw * (1.0 - lambda_init)


def _diffattn_kernel(q_ref, k_ref, v_ref, lq1_ref, lk1_ref, lq2_ref, lk2_ref, nw_ref, o_ref,
                     *, bq, lambda_init):
    qi = pl.program_id(2)
    q = q_ref[0]
    lane = lax.broadcasted_iota(jnp.int32, q.shape, 1)
    zero = jnp.zeros_like(q)
    qq = jnp.concatenate([jnp.where(lane < DIFF_DK, q, zero), jnp.where(lane >= DIFF_DK, q, zero)], axis=0)

    def step(j, carry, masked):
        m, l, acc = carry
        off = pl.multiple_of(j * bq, bq)
        kb = k_ref[0, pl.ds(off, bq), :]
        vb = v_ref[0, pl.ds(off, bq), :]
        s = _dot_nt(qq, kb)
        if masked:
            r = lax.broadcasted_iota(jnp.int32, s.shape, 0)
            r = jnp.where(r >= bq, r - bq, r)
            cidx = lax.broadcasted_iota(jnp.int32, s.shape, 1)
            s = jnp.where(cidx <= r, s, -jnp.inf)
        m_new = jnp.maximum(m, jnp.max(s, axis=-1, keepdims=True))
        p = jnp.exp(s - m_new)
        alpha = jnp.exp(m - m_new)
        l = alpha * l + jnp.sum(p, axis=-1, keepdims=True)
        acc = alpha * acc + _dot(p.astype(BF16), vb)
        return m_new, l, acc

    init = (jnp.full((2 * bq, 1), -jnp.inf, F32), jnp.zeros((2 * bq, 1), F32),
            jnp.zeros((2 * bq, DIFF_DV), F32))
    carry = lax.fori_loop(0, qi, functools.partial(step, masked=False), init)
    _, l, acc = step(qi, carry, True)
    on = acc / l
    lam = _diff_lambda(lq1_ref, lk1_ref, lq2_ref, lk2_ref, lambda_init)
    o = on[:bq] - lam * on[bq:]
    o_ref[0] = _rms_head(o, nw_ref[...], lambda_init).astype(o_ref.dtype)


def _diffattn(dq, dkb, dvb, lam_params, norm_w, lambda_init, bq):
    b, s, _ = dq.shape
    vec = _const_spec((1, DIFF_DK))
    return pl.pallas_call(
        functools.partial(_diffattn_kernel, bq=bq, lambda_init=lambda_init),
        out_shape=jax.ShapeDtypeStruct((b, s, DIFF_V_W), BF16),
        grid=(b, DIFF_HEADS, s // bq),
        in_specs=[pl.BlockSpec((1, bq, 2 * DIFF_DK), lambda i, h, j: (i, j, h)),
                  pl.BlockSpec((1, s, 2 * DIFF_DK), lambda i, h, j: (i, 0, h)),
                  pl.BlockSpec((1, s, DIFF_DV), lambda i, h, j: (i, 0, h)),
                  vec, vec, vec, vec, _const_spec((1, DIFF_DV))],
        out_specs=pl.BlockSpec((1, bq, DIFF_DV), lambda i, h, j: (i, j, h)),
        compiler_params=_cparams(("parallel", "parallel", "arbitrary")),
        name="diffattn",
    )(dq, dkb, dvb, *lam_params, norm_w)


def _decode_kernel(pt_ref, *refs, npg, n_steps, lambda_init):
    k_refs = refs[:npg]
    v_refs = refs[npg:2 * npg]
    (q_ref, kn_ref, vn_ref, lq1_ref, lk1_ref, lq2_ref, lk2_ref, nw_ref,
     o_ref, m_ref, l_ref, acc_ref) = refs[2 * npg:]
    j = pl.program_id(1)

    @pl.when(j == 0)
    def _():
        m_ref[...] = jnp.full(m_ref.shape, -jnp.inf, F32)
        l_ref[...] = jnp.zeros(l_ref.shape, F32)
        acc_ref[...] = jnp.zeros(acc_ref.shape, F32)

    q = q_ref[0]

    def update(kt, pv, mask):
        s = _dot(q, kt)
        if mask is not None:
            s = jnp.where(mask, s, -jnp.inf)
        m = m_ref[...]
        m_new = jnp.maximum(m, jnp.max(s, axis=-1, keepdims=True))
        p = jnp.exp(s - m_new)
        alpha = jnp.exp(m - m_new)
        l_ref[...] = alpha * l_ref[...] + jnp.sum(p, axis=-1, keepdims=True)
        acc_ref[...] = alpha * acc_ref[...] + pv(p.astype(BF16))
        m_ref[...] = m_new

    def pages_pv(p):
        def head_values(h):
            return jnp.concatenate(
                [v_ref[0, pl.ds(h, PAGE_SIZE, stride=DIFF_HEADS), :].astype(BF16) for v_ref in v_refs], axis=0)

        return jnp.concatenate([_dot(p, head_values(h)) for h in range(DIFF_HEADS)], axis=1)

    update(jnp.concatenate([k_ref[0].astype(BF16) for k_ref in k_refs], axis=1), pages_pv, None)

    @pl.when(j == n_steps - 1)
    def _():
        rows = 4 * 8
        r = lax.broadcasted_iota(jnp.int32, (rows, PAGE_SIZE), 0)
        cidx = lax.broadcasted_iota(jnp.int32, (rows, PAGE_SIZE), 1)
        update(kn_ref[0], lambda p: _dot(p, vn_ref[0]), cidx <= (r >> 3))
        lam = _diff_lambda(lq1_ref, lk1_ref, lq2_ref, lk2_ref, lambda_init)
        on = acc_ref[...] / l_ref[...]
        rr = lax.broadcasted_iota(jnp.int32, on.shape, 0)
        cc = lax.broadcasted_iota(jnp.int32, on.shape, 1)
        g = rr & 7
        coef = jnp.where((g & 1) == 0, 1.0, -lam)
        w = jnp.where((cc >> 7) == (g >> 1), coef, 0.0)
        o = jnp.sum((on * w).reshape(4, 8, DIFF_V_W), axis=1)
        for h in range(DIFF_HEADS):
            sl = slice(h * DIFF_DV, (h + 1) * DIFF_DV)
            o_ref[0, :, sl] = _rms_head(o[:, sl], nw_ref[...], lambda_init)


def _decode_attn(page_table, cache_kt, cache_v, qbd, kt_new, v_new, lam_params, norm_w, lambda_init):
    bd, n_pages = page_table.shape
    npg = math.gcd(n_pages, PAGES_PER_STEP)
    n_steps = n_pages // npg

    def page_spec(i):
        return pl.BlockSpec((1, DIFF_QK_W, PAGE_SIZE), lambda b, j, pt: (pt[b, j * npg + i], 0, 0))

    per_seq = lambda r, c: pl.BlockSpec((1, r, c), lambda b, j, pt: (b, 0, 0))
    vec = pl.BlockSpec((1, DIFF_DK), lambda b, j, pt: (0, 0))
    grid_spec = pltpu.PrefetchScalarGridSpec(
        num_scalar_prefetch=1,
        grid=(bd, n_steps),
        in_specs=[page_spec(i) for i in range(npg)] + [page_spec(i) for i in range(npg)]
        + [per_seq(32, DIFF_QK_W), per_seq(DIFF_QK_W, PAGE_SIZE), per_seq(PAGE_SIZE, DIFF_V_W), vec, vec, vec, vec,
           pl.BlockSpec((1, DIFF_DV), lambda b, j, pt: (0, 0))],
        out_specs=per_seq(4, DIFF_V_W),
        scratch_shapes=[pltpu.VMEM((32, 1), F32), pltpu.VMEM((32, 1), F32), pltpu.VMEM((32, DIFF_V_W), F32)],
    )
    return pl.pallas_call(
        functools.partial(_decode_kernel, npg=npg, n_steps=n_steps, lambda_init=lambda_init),
        out_shape=jax.ShapeDtypeStruct((bd, 4, DIFF_V_W), F32),
        grid_spec=grid_spec,
        compiler_params=_cparams(("parallel", "arbitrary")),
        name="decode_attn",
    )(page_table, *([cache_kt] * npg), *([cache_v] * npg), qbd, kt_new, v_new, *lam_params, norm_w)


def _layer_norm(x, w, b):
    mu = jnp.mean(x, axis=-1, keepdims=True)
    xc = x - mu
    var = jnp.mean(xc * xc, axis=-1, keepdims=True)
    return xc * lax.rsqrt(var + LN_EPS) * w + b


def _tail_kernel(x_ref, on_ref, od_ref, wg_ref, rnw_ref, wur_ref, wud_ref, wo_ref, l1w_ref, l1b_ref,
                 sw1_ref, sw3_ref, sw2_ref, h_ref, base_ref, *, alpha):
    x = x_ref[...]
    xb = x.astype(BF16)
    rg = _dot(xb, wg_ref[:, 0:1024])
    o_ret = _silu(rg) * (on_ref[...].astype(F32) * rnw_ref[...])
    u = _sigmoid(_dot(xb, wg_ref[:, 1024:2048])) * _dot(o_ret.astype(BF16), wur_ref[...])
    u = u + _sigmoid(_dot(xb, wg_ref[:, 2048:3072])) * _dot(od_ref[...], wud_ref[...])
    h = _layer_norm(alpha * x + _dot(u.astype(BF16), wo_ref[...]), l1w_ref[...], l1b_ref[...])
    hb = h.astype(BF16)
    a = _silu(_dot(hb, sw1_ref[...])) * _dot(hb, sw3_ref[...])
    _store_slabs(h_ref, _pack_row(h))
    base_ref[...] = alpha * h + _dot(a.astype(BF16), sw2_ref[...])


def _tail(x2d, on, od, w, alpha, tm):
    t = x2d.shape[0]
    row = lambda wd: pl.BlockSpec((tm, wd), lambda i: (i, 0))
    names = ("wg", "ret_norm_w", "w_up_ret", "w_up_diff", "w_out", "ln1_w", "ln1_b", "sw1", "sw3", "sw2")
    ws = [w[n] for n in names]
    return pl.pallas_call(
        functools.partial(_tail_kernel, alpha=alpha),
        out_shape=(jax.ShapeDtypeStruct((t * ROW_SLABS, LANES), jnp.uint32),
                   jax.ShapeDtypeStruct((t, D_MODEL), F32)),
        grid=(t // tm,),
        in_specs=[row(D_MODEL), row(RET_V_W), row(DIFF_V_W)] + [_const_spec(a.shape) for a in ws],
        out_specs=(pl.BlockSpec((tm * ROW_SLABS, LANES), lambda i: (i, 0)), row(D_MODEL)),
        compiler_params=_cparams(("parallel",)),
        name="tail",
    )(x2d, on, od, *ws)


def _first_argmax(x, iota, size):
    m = jnp.max(x, axis=0, keepdims=True)
    idx = jnp.min(jnp.where(x == m, iota, size), axis=0, keepdims=True)
    return m, idx


def _router_kernel(h_ref, rw_ref, rb_ref, eidx_ref, gate_ref, pos_ref, cnt_ref, carry_ref):
    i = pl.program_id(0)
    tb = h_ref.shape[0] // ROW_SLABS
    per = N_EXPERTS // N_GROUPS

    @pl.when(i == 0)
    def _():
        carry_ref[...] = jnp.zeros(carry_ref.shape, F32)

    s = _sigmoid(_dot_nt(rw_ref[...], _unpack_row(_load_slabs(h_ref, tb)).astype(BF16)))
    sb = s + rb_ref[...]
    neg = -jnp.inf

    sb3 = sb.reshape(N_GROUPS, per, tb)
    io3 = lax.broadcasted_iota(jnp.int32, sb3.shape, 1)
    m1 = jnp.max(sb3, axis=1, keepdims=True)
    i1 = jnp.min(jnp.where(sb3 == m1, io3, per), axis=1, keepdims=True)
    m2 = jnp.max(jnp.where(io3 == i1, neg, sb3), axis=1, keepdims=True)
    gscore = (m1 + m2).reshape(N_GROUPS, tb)

    iog = lax.broadcasted_iota(jnp.int32, gscore.shape, 0)
    gsel = jnp.zeros(gscore.shape, F32)
    for _ in range(TOPK_GROUPS):
        _, gi = _first_argmax(gscore, iog, N_GROUPS)
        hit = iog == gi
        gsel = jnp.where(hit, 1.0, gsel)
        gscore = jnp.where(hit, neg, gscore)

    emask = jnp.broadcast_to(gsel.reshape(N_GROUPS, 1, tb), (N_GROUPS, per, tb)).reshape(N_EXPERTS, tb)
    cand = jnp.where(emask > 0.0, sb, neg)
    ioe = lax.broadcasted_iota(jnp.int32, cand.shape, 0)
    chosen = jnp.zeros(cand.shape, F32)
    idxs, gates = [], []
    for _ in range(TOP_K):
        _, ei = _first_argmax(cand, ioe, N_EXPERTS)
        hit = ioe == ei
        gates.append(jnp.sum(jnp.where(hit, s, 0.0), axis=0, keepdims=True))
        idxs.append(ei)
        chosen = jnp.where(hit, 1.0, chosen)
        cand = jnp.where(hit, neg, cand)

    gsum = gates[0]
    for g in gates[1:]:
        gsum = gsum + g
    scale = ROUTED_SCALE / gsum
    for k in range(TOP_K):
        eidx_ref[k:k + 1, :] = idxs[k]
        gate_ref[k:k + 1, :] = gates[k] * scale

    r = lax.broadcasted_iota(jnp.int32, (tb, tb), 0)
    cidx = lax.broadcasted_iota(jnp.int32, (tb, tb), 1)
    before = jnp.where(r < cidx, 1.0, 0.0).astype(BF16)
    rank = carry_ref[...] + _dot(chosen.astype(BF16), before)
    for k in range(TOP_K):
        pk = jnp.sum(jnp.where(ioe == idxs[k], rank, 0.0), axis=0, keepdims=True)
        pos_ref[k:k + 1, :] = pk.astype(jnp.int32)
    total = carry_ref[...] + jnp.sum(chosen, axis=1, keepdims=True)
    carry_ref[...] = total
    cnt_ref[...] = total.astype(jnp.int32)


def _router(h, rw_t, rb_col, tb):
    t = h.shape[0] // ROW_SLABS
    slot = pl.BlockSpec((TOP_K, tb), lambda i: (0, i))
    return pl.pallas_call(
        _router_kernel,
        out_shape=(jax.ShapeDtypeStruct((TOP_K, t), jnp.int32), jax.ShapeDtypeStruct((TOP_K, t), F32),
                   jax.ShapeDtypeStruct((TOP_K, t), jnp.int32), jax.ShapeDtypeStruct((N_EXPERTS, 1), jnp.int32)),
        grid=(t // tb,),
        in_specs=[pl.BlockSpec((tb * ROW_SLABS, LANES), lambda i: (i, 0)), _const_spec(rw_t.shape),
                  _const_spec(rb_col.shape)],
        out_specs=(slot, slot, slot, _const_spec((N_EXPERTS, 1))),
        scratch_shapes=[pltpu.VMEM((N_EXPERTS, 1), F32)],
        compiler_params=_cparams(("arbitrary",)),
        name="router",
    )(h, rw_t, rb_col)


def _row_copy_wait(src_rows, dst_rows, sem):
    pltpu.make_async_copy(src_rows, dst_rows, sem).wait()


_PAD_CHUNKS = tuple(1 << s for s in reversed(range(MOE_BLOCK.bit_length() - 1)))


def _zero_fill_padding(ps_ref, cnt_ref, zero_ref, xs_ref, sem):
    def chunks(e, act):
        cnt = cnt_ref[e]
        n_pad = (MOE_BLOCK - (cnt & (MOE_BLOCK - 1))) & (MOE_BLOCK - 1)
        off = ps_ref[e] + cnt
        for rows in _PAD_CHUNKS:
            take = n_pad & rows

            @pl.when(take != 0)
            def _(off=off, rows=rows):
                act(pltpu.make_async_copy(zero_ref.at[pl.ds(0, rows)], xs_ref.at[pl.ds(off, rows)], sem))

            off = off + take

    def start(e, carry):
        chunks(e, lambda cp: cp.start())
        return carry

    def wait(e, carry):
        chunks(e, lambda cp: cp.wait())
        return carry

    lax.fori_loop(0, N_EXPERTS, start, 0)
    lax.fori_loop(0, N_EXPERTS, wait, 0)


def _zero_fill_unused_blocks(nb_ref, zero_ref, xs_ref, sem):
    rows = zero_ref.shape[0]
    n_total = xs_ref.shape[0] // MOE_BLOCK

    def copies(blk, act):
        for j in range(MOE_BLOCK // rows):
            act(pltpu.make_async_copy(zero_ref, xs_ref.at[pl.ds(blk * MOE_BLOCK + j * rows, rows)], sem))

    def start(blk, carry):
        copies(blk, lambda cp: cp.start())
        return carry

    def wait(blk, carry):
        copies(blk, lambda cp: cp.wait())
        return carry

    lax.fori_loop(nb_ref[0], n_total, start, 0)
    lax.fori_loop(nb_ref[0], n_total, wait, 0)


def _dest_kernel(ps_ref, eidx_ref, pos_ref, dest_ref):
    ps = ps_ref[...]
    ioe = lax.broadcasted_iota(jnp.int32, (N_EXPERTS, eidx_ref.shape[1]), 0)
    for k in range(TOP_K):
        start = jnp.sum(jnp.where(ioe == eidx_ref[k:k + 1, :], ps, 0.0), axis=0, keepdims=True)
        dest_ref[k:k + 1, :] = start.astype(jnp.int32) + pos_ref[k:k + 1, :]


def _dest_rows(pad_start, eidx, pos, tb):
    t = eidx.shape[1]
    slot = pl.BlockSpec((TOP_K, tb), lambda i: (0, i))
    return pl.pallas_call(
        _dest_kernel,
        out_shape=jax.ShapeDtypeStruct((TOP_K, t), jnp.int32),
        grid=(t // tb,),
        in_specs=[_const_spec((N_EXPERTS, 1)), slot, slot],
        out_specs=slot,
        compiler_params=_cparams(("parallel",)),
        name="dest_rows",
    )(pad_start.astype(F32).reshape(N_EXPERTS, 1), eidx, pos)


def _dispatch_kernel(ps_ref, cnt_ref, nb_ref, dest_ref, h_ref, xs_ref, zero_ref, sem, zsem):
    per_tile = SUBLANES // ROW_SLABS
    tm = h_ref.shape[0] // ROW_SLABS

    @pl.when(pl.program_id(0) == 0)
    def _():
        zero_ref[...] = jnp.zeros(zero_ref.shape, zero_ref.dtype)
        _zero_fill_padding(ps_ref, cnt_ref, zero_ref, xs_ref, zsem)
        _zero_fill_unused_blocks(nb_ref, zero_ref, xs_ref, zsem)

    def body(u, carry):
        for par in range(per_tile):
            t = u * per_tile + par
            for k in range(TOP_K):
                pltpu.make_async_copy(_slab(h_ref, u, par), xs_ref.at[dest_ref[k, t]], sem).start(priority=k % 2)
        return carry

    lax.fori_loop(0, tm // per_tile, body, 0)
    for k in range(TOP_K):
        _row_copy_wait(h_ref, h_ref, sem)


def _dispatch(pad_start, counts, n_used, dest, h, n_rows, tm):
    t = h.shape[0] // ROW_SLABS
    slot = pl.BlockSpec((TOP_K, tm), lambda i, ps, cnt, nb: (0, i), memory_space=pltpu.SMEM)
    grid_spec = pltpu.PrefetchScalarGridSpec(
        num_scalar_prefetch=3,
        grid=(t // tm,),
        in_specs=[slot, pl.BlockSpec((tm * ROW_SLABS, LANES), lambda i, ps, cnt, nb: (i, 0))],
        out_specs=pl.BlockSpec(memory_space=pl.ANY),
        scratch_shapes=[pltpu.VMEM((_PAD_CHUNKS[0], ROW_SLABS, LANES), jnp.uint32), pltpu.SemaphoreType.DMA(()),
                        pltpu.SemaphoreType.DMA(())],
    )
    return pl.pallas_call(
        _dispatch_kernel,
        out_shape=jax.ShapeDtypeStruct((n_rows, ROW_SLABS, LANES), jnp.uint32),
        grid_spec=grid_spec,
        compiler_params=_cparams(("arbitrary",)),
        name="dispatch",
    )(pad_start, counts, n_used, dest, h)


def _expert_kernel(be_ref, nb_ref, xs_ref, w1_ref, w3_ref, w2_ref, ys_ref, w1b_ref, w3b_ref, w2b_ref):
    i = pl.program_id(0)

    @pl.when(jnp.logical_or(i == 0, be_ref[i] != be_ref[jnp.maximum(i - 1, 0)]))
    def _():
        w1b_ref[...] = w1_ref[0].astype(BF16)
        w3b_ref[...] = w3_ref[0].astype(BF16)
        w2b_ref[...] = w2_ref[0].astype(BF16)

    @pl.when(i < nb_ref[0])
    def _():
        xb = _unpack_row(_load_slabs(xs_ref, MOE_BLOCK)).astype(BF16)
        a = _silu(_dot(xb, w1b_ref[...])) * _dot(xb, w3b_ref[...])
        _store_slabs(ys_ref, _pack_row(_dot(a.astype(BF16), w2b_ref[...])))

    @pl.when(i >= nb_ref[0])
    def _():
        ys_ref[...] = jnp.zeros(ys_ref.shape, ys_ref.dtype)


def _experts(block_expert, n_used, xs, w1, w3, w2):
    n_rows = xs.shape[0] // ROW_SLABS
    grid_spec = pltpu.PrefetchScalarGridSpec(
        num_scalar_prefetch=2,
        grid=(n_rows // MOE_BLOCK,),
        in_specs=[pl.BlockSpec((MOE_BLOCK * ROW_SLABS, LANES), lambda i, be, nb: (jnp.minimum(i, nb[0] - 1), 0)),
                  pl.BlockSpec((1, D_MODEL, D_EXPERT), lambda i, be, nb: (be[i], 0, 0)),
                  pl.BlockSpec((1, D_MODEL, D_EXPERT), lambda i, be, nb: (be[i], 0, 0)),
                  pl.BlockSpec((1, D_EXPERT, D_MODEL), lambda i, be, nb: (be[i], 0, 0))],
        out_specs=pl.BlockSpec((MOE_BLOCK * ROW_SLABS, LANES), lambda i, be, nb: (i, 0)),
        scratch_shapes=[pltpu.VMEM((D_MODEL, D_EXPERT), BF16), pltpu.VMEM((D_MODEL, D_EXPERT), BF16),
                        pltpu.VMEM((D_EXPERT, D_MODEL), BF16)],
    )
    return pl.pallas_call(
        _expert_kernel,
        out_shape=jax.ShapeDtypeStruct((n_rows * ROW_SLABS, LANES), jnp.uint32),
        grid_spec=grid_spec,
        compiler_params=_cparams(("arbitrary",)),
        name="experts",
    )(block_expert, n_used, xs, w1, w3, w2)


def _combine_kernel(dcur_ref, dnxt_ref, base_ref, gate_ref, p_ref, ys_ref,
                    l2w_ref, l2b_ref, wpg_ref, wpe_ref, y_ref, rows_ref, sems):
    i = pl.program_id(0)
    n = pl.num_programs(0)
    tm = base_ref.shape[0]

    per_tile = SUBLANES // ROW_SLABS

    def start_gather(d_ref, slot):
        def body(u, carry):
            for par in range(per_tile):
                t = u * per_tile + par
                for k in range(TOP_K):
                    pltpu.make_async_copy(ys_ref.at[d_ref[k, t]], _slab(rows_ref.at[slot, k], u, par),
                                          sems.at[slot]).start(priority=k % 2)
            return carry

        lax.fori_loop(0, tm // per_tile, body, 0)

    @pl.when(i == 0)
    def _():
        start_gather(dcur_ref, 0)

    slot = i % 2
    n_pairs = tm // per_tile

    def start_gather_piece(d_ref, s, piece):
        for u in range(piece * n_pairs // TOP_K, (piece + 1) * n_pairs // TOP_K):
            for par in range(per_tile):
                lo = u * SUBLANES + par * ROW_SLABS
                for k in range(TOP_K):
                    pltpu.make_async_copy(ys_ref.at[d_ref[k, u * per_tile + par]],
                                          rows_ref.at[s, k, lo:lo + ROW_SLABS], sems.at[s]).start(priority=k % 2)

    def wait_rows(s):
        for k in range(TOP_K):
            _row_copy_wait(rows_ref.at[s, k], rows_ref.at[s, k], sems.at[s])

    def reduce_and_finish(s, issue_piece):
        g = gate_ref[...]
        m = base_ref[...]
        for k in range(TOP_K):
            issue_piece(k)
            m = m + _unpack_row(_load_slabs(rows_ref.at[s, k], tm)) * g[:, k:k + 1]
        h2 = _layer_norm(m, l2w_ref[...], l2b_ref[...])
        emb = _dot(p_ref[...].astype(BF16), wpe_ref[...])
        y_ref[...] = h2 + _sigmoid(_dot(h2.astype(BF16), wpg_ref[...])) * emb

    for s in range(2):
        @pl.when(jnp.logical_and(slot == s, i + 1 < n))
        def _(s=s):
            wait_rows(s)
            reduce_and_finish(s, lambda piece: start_gather_piece(dnxt_ref, 1 - s, piece))

        @pl.when(jnp.logical_and(slot == s, i + 1 == n))
        def _(s=s):
            wait_rows(s)
            reduce_and_finish(s, lambda piece: None)


def _combine(dest, base, gate_t, p2d, ys, w, tm):
    t = base.shape[0]
    n = t // tm
    row = lambda wd: pl.BlockSpec((tm, wd), lambda i: (i, 0))
    cur = pl.BlockSpec((TOP_K, tm), lambda i: (0, i), memory_space=pltpu.SMEM)
    nxt = pl.BlockSpec((TOP_K, tm), lambda i: (0, jnp.minimum(i + 1, n - 1)), memory_space=pltpu.SMEM)
    ws = [w[nm] for nm in ("ln2_w", "ln2_b", "w_pg", "w_pe")]
    return pl.pallas_call(
        _combine_kernel,
        out_shape=jax.ShapeDtypeStruct((t, D_MODEL), F32),
        grid=(n,),
        in_specs=[cur, nxt, row(D_MODEL), row(TOP_K), row(p2d.shape[1]), pl.BlockSpec(memory_space=pl.ANY)]
        + [_const_spec(a.shape) for a in ws],
        out_specs=row(D_MODEL),
        scratch_shapes=[pltpu.VMEM((2, TOP_K, tm * ROW_SLABS, LANES), jnp.uint32), pltpu.SemaphoreType.DMA((2,))],
        compiler_params=_cparams(("arbitrary",)),
        name="combine",
    )(dest, dest, base, gate_t, p2d, ys, *ws)


def _moe_and_out(h, base, p2d, w, tb, tm_disp, tm_comb):
    t = h.shape[0] // ROW_SLABS
    eidx, gate, pos, counts = _router(h, w["router_wt"], w["router_b"], tb)
    counts = counts[:, 0]
    padded = (counts + MOE_BLOCK - 1) // MOE_BLOCK * MOE_BLOCK
    pad_end = jnp.cumsum(padded)
    pad_start = pad_end - padded
    n_blocks = (t * TOP_K + N_EXPERTS * (MOE_BLOCK - 1) + MOE_BLOCK - 1) // MOE_BLOCK
    block_start = jnp.arange(n_blocks, dtype=jnp.int32) * MOE_BLOCK
    block_expert = jnp.minimum(jnp.sum((pad_end[None, :] <= block_start[:, None]).astype(jnp.int32), axis=1),
                               N_EXPERTS - 1)
    n_used = (pad_end[-1:] // MOE_BLOCK).astype(jnp.int32)
    dest = _dest_rows(pad_start, eidx, pos, tb)
    xs = _dispatch(pad_start, counts, n_used, dest, h, n_blocks * MOE_BLOCK, tm_disp)
    ys = _experts(block_expert, n_used, xs.reshape(-1, LANES), w["ew1"], w["ew3"], w["ew2"])
    return _combine(dest, base, gate.T, p2d, ys.reshape(-1, ROW_SLABS, LANES), w, tm_comb)


def _layer_weights(i, w_in, ret_norm_w, diff_norm_w, w_up_ret, w_up_diff, w_out, ln1_w, ln1_b, router_w, router_b,
                   expert_w1, expert_w3, expert_w2, shared_w1, shared_w3, shared_w2, ln2_w, ln2_b, w_pe, w_pg):
    o = _OFFS
    wi = w_in[i]
    vec = lambda a: a[i].reshape(1, -1).astype(F32)
    return {
        "w_mix": jnp.concatenate([wi[:, o[0]:o[3]], wi[:, o[4]:o[7]]], axis=1).astype(BF16),
        "wg": jnp.concatenate([wi[:, o[3]:o[4]], wi[:, o[7]:o[9]]], axis=1).astype(BF16),
        "ret_norm_w": vec(ret_norm_w), "diff_norm_w": vec(diff_norm_w),
        "w_up_ret": w_up_ret[i].astype(BF16), "w_up_diff": w_up_diff[i].astype(BF16), "w_out": w_out[i].astype(BF16),
        "ln1_w": vec(ln1_w), "ln1_b": vec(ln1_b), "ln2_w": vec(ln2_w), "ln2_b": vec(ln2_b),
        "router_wt": router_w[i].T.astype(BF16), "router_b": router_b[i].reshape(-1, 1).astype(F32),
        "ew1": expert_w1[i], "ew3": expert_w3[i], "ew2": expert_w2[i],
        "sw1": shared_w1[i].astype(BF16), "sw3": shared_w3[i].astype(BF16), "sw2": shared_w2[i].astype(BF16),
        "w_pe": w_pe[i].astype(BF16), "w_pg": w_pg[i].astype(BF16),
    }


def _pick(n, pref):
    return pref if n % pref == 0 else n


def kernel(x_prompt, x_sample, cache_diff_k, cache_diff_v, state_retention, page_table, p_prompt, p_sample, w_in, ret_norm_w, diff_lq1, diff_lk1, diff_lq2, diff_lk2, diff_norm_w, w_up_ret, w_up_diff, w_out, ln1_w, ln1_b, router_w, router_b, expert_w1, expert_w3, expert_w2, shared_w1, shared_w3, shared_w2, ln2_w, ln2_b, w_pe, w_pg):
    depth = w_in.shape[0]
    b, s, d = x_prompt.shape
    bd, ls, _ = x_sample.shape
    n_pages = page_table.shape[1]
    past = n_pages * PAGE_SIZE
    alpha = (2 * depth) ** 0.25
    tp, ts = b * s, bd * ls

    cos_p, sin_p = _rotation_tables(jnp.arange(s))
    cos_s, sin_s = _rotation_tables(past + jnp.arange(ls))
    cos_s = jnp.tile(cos_s, (bd, 1))
    sin_s = jnp.tile(sin_s, (bd, 1))

    yp, ys = x_prompt.reshape(tp, d), x_sample.reshape(ts, d)
    kp_l, vp_l, sp_l, ks_l, vs_l, ss_l = [], [], [], [], [], []
    for i in range(depth):
        lambda_init = 0.8 - 0.6 * math.exp(-0.3 * i)
        w = _layer_weights(i, w_in, ret_norm_w, diff_norm_w, w_up_ret, w_up_diff, w_out, ln1_w, ln1_b, router_w,
                           router_b, expert_w1, expert_w3, expert_w2, shared_w1, shared_w3, shared_w2,
                           ln2_w, ln2_b, w_pe, w_pg)
        lam_params = tuple(a[i].reshape(1, DIFF_DK).astype(F32) for a in (diff_lq1, diff_lk1, diff_lq2, diff_lk2))

        rq, rk, rv, dq, dk, dv, dkb, dvb = _proj(yp, w["w_mix"], cos_p, sin_p, _pick(s, 512))
        seq = lambda a: a.reshape(b, s, a.shape[-1])
        on, st_p = _retention(seq(rq), seq(rk), seq(rv), jnp.zeros((b, RET_HEADS, RET_DK, RET_DV), F32),
                              RET_CHUNK, _pick(s, 512))
        od = _diffattn(seq(dq), seq(dkb), seq(dvb), lam_params, w["diff_norm_w"], lambda_init, _pick(s, 512))
        h, base = _tail(yp, on.reshape(tp, RET_V_W), od.reshape(tp, DIFF_V_W), w, alpha, _pick(tp, 256))
        yp = _moe_and_out(h, base, p_prompt[i].reshape(tp, -1), w, _pick(tp, 512), _pick(tp, 256), _pick(tp, 128))
        kp_l.append(dk.reshape(b, s, DIFF_HEADS, 2, DIFF_DK))
        vp_l.append(dv.reshape(b, s, DIFF_HEADS, DIFF_DV))
        sp_l.append(st_p)

        rq, rk, rv, dq, dk, dv, dkb, dvb = _proj(ys, w["w_mix"], cos_s, sin_s, _pick(ts, 512))
        pad_rows = lambda a, r: jnp.pad(a.reshape(bd, ls, a.shape[-1]), ((0, 0), (0, r - ls), (0, 0)))
        on, st_s = _retention(pad_rows(rq, RET_CHUNK), pad_rows(rk, RET_CHUNK), pad_rows(rv, RET_CHUNK),
                              state_retention[i], ls, RET_CHUNK)
        on = on[:, :ls].reshape(ts, RET_V_W)
        q5 = dq.reshape(bd, ls, 1, 2 * DIFF_HEADS, DIFF_DK)
        eye = jnp.eye(2 * DIFF_HEADS, dtype=BF16)[None, None, :, :, None]
        qbd = (q5 * eye).reshape(bd, ls * 2 * DIFF_HEADS, DIFF_QK_W)
        cache_kt = jnp.transpose(cache_diff_k[i], (0, 2, 3, 4, 1)).reshape(-1, DIFF_QK_W, PAGE_SIZE)
        cache_v = cache_diff_v[i].reshape(-1, PAGE_SIZE * DIFF_HEADS, DIFF_DV)
        kt_new = jnp.pad(jnp.swapaxes(dkb.reshape(bd, ls, DIFF_QK_W), 1, 2), ((0, 0), (0, 0), (0, PAGE_SIZE - ls)))
        od = _decode_attn(page_table, cache_kt, cache_v, qbd, kt_new, pad_rows(dvb, PAGE_SIZE), lam_params,
                          w["diff_norm_w"], lambda_init)
        od = od.reshape(ts, DIFF_V_W).astype(BF16)
        h, base = _tail(ys, on, od, w, alpha, _pick(ts, 256))
        ys = _moe_and_out(h, base, p_sample[i].reshape(ts, -1), w, _pick(ts, 512), _pick(ts, 256), _pick(ts, 128))
        ks_l.append(dk.reshape(bd, ls, DIFF_HEADS, 2, DIFF_DK))
        vs_l.append(dv.reshape(bd, ls, DIFF_HEADS, DIFF_DV))
        ss_l.append(st_s)

    return (yp.reshape(b, s, d), ys.reshape(bd, ls, d), jnp.stack(kp_l), jnp.stack(vp_l), jnp.stack(sp_l),
            jnp.stack(ks_l), jnp.stack(vs_l), jnp.stack(ss_l))
```

```python
import functools
import math

import jax
import jax.numpy as jnp
import numpy as np
from jax import lax
from jax.experimental import pallas as pl
from jax.experimental.pallas import tpu as pltpu

F32 = jnp.float32
BF16 = jnp.bfloat16

D_MODEL = 1024
PAGE_SIZE = 128
RET_HEADS = 4
RET_DK = 128
RET_DV = 256
RET_CHUNK = 128
DIFF_HEADS = 4
DIFF_DK = 64
DIFF_DV = 128
N_EXPERTS = 256
TOP_K = 8
N_GROUPS = 8
TOPK_GROUPS = 4
D_EXPERT = 256
ROUTED_SCALE = 2.5
LN_EPS = 1e-5

RET_QK_W = RET_HEADS * RET_DK
RET_V_W = RET_HEADS * RET_DV
DIFF_QK_W = DIFF_HEADS * 2 * DIFF_DK
DIFF_V_W = DIFF_HEADS * DIFF_DV
_SPLITS = (RET_QK_W, RET_QK_W, RET_V_W, RET_V_W, DIFF_QK_W, DIFF_QK_W, DIFF_V_W, D_MODEL, D_MODEL)
_OFFS = tuple(int(o) for o in np.cumsum((0,) + _SPLITS))

LANES = 128
SUBLANES = 8
VMEM_LIMIT = 56 * 1024 * 1024

MOE_BLOCK = 256
PAGES_PER_STEP = 32


def _cparams(sem, vmem=VMEM_LIMIT):
    return pltpu.CompilerParams(dimension_semantics=sem, vmem_limit_bytes=vmem)


def _const_spec(shape):
    nd = len(shape)
    return pl.BlockSpec(shape, lambda *_: (0,) * nd)


def _sigmoid(x):
    return 1.0 / (1.0 + jnp.exp(-x))


def _silu(x):
    return x * _sigmoid(x)


PACK_W = D_MODEL // 2
ROW_SLABS = PACK_W // LANES
_HI_MASK = np.uint32(0xFFFF0000)


def _pack_row(x):
    bits = lambda v: lax.bitcast_convert_type(v.astype(BF16).astype(F32), jnp.uint32)
    return (bits(x[:, :PACK_W]) >> 16) | (bits(x[:, PACK_W:]) & _HI_MASK)


def _unpack_row(p):
    lo = lax.bitcast_convert_type(p << 16, F32)
    hi = lax.bitcast_convert_type(p & _HI_MASK, F32)
    return jnp.concatenate([lo, hi], axis=1)


def _store_slabs(ref, packed):
    rows = packed.shape[0]
    for j in range(ROW_SLABS):
        ref[pl.ds(j, rows, stride=ROW_SLABS), :] = packed[:, j * LANES:(j + 1) * LANES]


def _load_slabs(ref, rows):
    return jnp.concatenate([ref[pl.ds(j, rows, stride=ROW_SLABS), :] for j in range(ROW_SLABS)], axis=1)


def _slab(ref, t_pair, par):
    tile = ref.at[pl.ds(pl.multiple_of(t_pair * SUBLANES, SUBLANES), SUBLANES)]
    return tile.at[par * ROW_SLABS:(par + 1) * ROW_SLABS]


def _dot(a, b):
    return jnp.dot(a, b, preferred_element_type=F32)


def _dot_nt(a, b):
    return lax.dot_general(a, b, (((1,), (1,)), ((), ())), preferred_element_type=F32)


def _dot_tn(a, b):
    return lax.dot_general(a, b, (((0,), (0,)), ((), ())), preferred_element_type=F32)


def _swap_pairs(x):
    lane = lax.broadcasted_iota(jnp.int32, x.shape, 1)
    nxt = pltpu.roll(x, LANES - 1, 1)
    prv = pltpu.roll(x, 1, 1)
    return jnp.where((lane & 1) == 0, nxt, prv)


def _proj_kernel(x_ref, w_ref, cos_ref, sin_ref,
                 rq_ref, rk_ref, rv_ref, dq_ref, dk_ref, dv_ref, dkb_ref, dvb_ref):
    xb = x_ref[...].astype(BF16)
    cos = cos_ref[...]
    sin = sin_ref[...]

    def mm(lo, hi):
        return _dot(xb, w_ref[:, lo:hi])

    q = mm(0, 512)
    k = mm(512, 1024)
    for h in range(RET_HEADS):
        sl = slice(h * RET_DK, (h + 1) * RET_DK)
        qh = q[:, sl]
        kh = k[:, sl]
        rq_ref[:, sl] = (qh * cos + _swap_pairs(qh) * sin).astype(BF16)
        rk_ref[:, sl] = ((kh * cos + _swap_pairs(kh) * sin) * (RET_DK ** -0.5)).astype(BF16)
    rv_ref[...] = mm(1024, 2048).astype(BF16)
    dq_ref[...] = (mm(2048, 2560) * (DIFF_DK ** -0.5)).astype(BF16)
    tm = x_ref.shape[0]
    dk = mm(2560, 3072)
    for g in range(2 * DIFF_HEADS):
        dk_ref[pl.ds(g, tm, stride=2 * DIFF_HEADS), :] = dk[:, g * DIFF_DK:(g + 1) * DIFF_DK]
    dkb_ref[...] = dk.astype(BF16)
    dv = mm(3072, 3584)
    for h in range(DIFF_HEADS):
        dv_ref[pl.ds(h, tm, stride=DIFF_HEADS), :] = dv[:, h * DIFF_DV:(h + 1) * DIFF_DV]
    dvb_ref[...] = dv.astype(BF16)


def _proj(x2d, w_mix, cos_t, sin_t, tm):
    t = x2d.shape[0]
    nt = cos_t.shape[0] // tm
    row = lambda w: pl.BlockSpec((tm, w), lambda i: (i, 0))
    tab = pl.BlockSpec((tm, LANES), lambda i: (i % nt, 0))
    out_shapes = (
        jax.ShapeDtypeStruct((t, RET_QK_W), BF16), jax.ShapeDtypeStruct((t, RET_QK_W), BF16),
        jax.ShapeDtypeStruct((t, RET_V_W), BF16), jax.ShapeDtypeStruct((t, DIFF_QK_W), BF16),
        jax.ShapeDtypeStruct((t * 2 * DIFF_HEADS, DIFF_DK), F32), jax.ShapeDtypeStruct((t * DIFF_HEADS, DIFF_DV), F32),
        jax.ShapeDtypeStruct((t, DIFF_QK_W), BF16), jax.ShapeDtypeStruct((t, DIFF_V_W), BF16),
    )
    k_rows = pl.BlockSpec((tm * 2 * DIFF_HEADS, DIFF_DK), lambda i: (i, 0))
    v_rows = pl.BlockSpec((tm * DIFF_HEADS, DIFF_DV), lambda i: (i, 0))
    return pl.pallas_call(
        _proj_kernel,
        out_shape=out_shapes,
        grid=(t // tm,),
        in_specs=[row(D_MODEL), _const_spec(w_mix.shape), tab, tab],
        out_specs=(row(512), row(512), row(1024), row(512), k_rows, v_rows, row(512), row(512)),
        compiler_params=_cparams(("parallel",)),
        name="proj",
    )(x2d, w_mix, cos_t, sin_t)


def _rotation_tables(pos):
    inv = 1.0 / (10000.0 ** jnp.linspace(0.0, 1.0, RET_DK // 2))
    ang = pos.astype(F32)[:, None] * inv[None, :]
    cos = jnp.repeat(jnp.cos(ang), 2, axis=1)
    sin = jnp.sin(ang)
    sin = jnp.stack([-sin, sin], axis=-1).reshape(pos.shape[0], RET_DK)
    return cos, sin


def _retention_kernel(q_ref, k_ref, v_ref, s0_ref, dm_ref, qd_ref, kd_ref, o_ref, s_ref, st_ref,
                      *, n_sub, gl):
    c = pl.program_id(1)

    @pl.when(c == 0)
    def _():
        st_ref[...] = s0_ref[0]

    for j in range(n_sub):
        rows = slice(j * RET_CHUNK, (j + 1) * RET_CHUNK)
        for h in range(RET_HEADS):
            q = q_ref[0, rows, h * RET_DK:(h + 1) * RET_DK]
            k = k_ref[0, rows, h * RET_DK:(h + 1) * RET_DK]
            v = v_ref[0, rows, h * RET_DV:(h + 1) * RET_DV]
            st = st_ref[h]
            qk = _dot_nt(q, k) * dm_ref[h]
            q_dec = (q.astype(F32) * qd_ref[h]).astype(BF16)
            o = _dot(qk.astype(BF16), v) + _dot(q_dec, st.astype(BF16))
            k_dec = (k.astype(F32) * kd_ref[h]).astype(BF16)
            st_ref[h] = gl[h] * st + _dot_tn(k_dec, v)
            mu = jnp.mean(o, axis=-1, keepdims=True)
            oc = o - mu
            var = jnp.mean(oc * oc, axis=-1, keepdims=True)
            o_ref[0, rows, h * RET_DV:(h + 1) * RET_DV] = (oc * lax.rsqrt(var + LN_EPS)).astype(BF16)

    @pl.when(c == pl.num_programs(1) - 1)
    def _():
        s_ref[0] = st_ref[...]


def _retention_tables(length):
    lg = np.log1p(-(2.0 ** (-5.0 - np.arange(RET_HEADS, dtype=np.float64))))
    idx = np.arange(RET_CHUNK, dtype=np.float64)
    rel = idx[:, None] - idx[None, :]
    valid = (idx < length)
    dm = np.where((rel >= 0) & valid[:, None] & valid[None, :],
                  np.exp(np.maximum(rel, 0.0)[None] * lg[:, None, None]), 0.0)
    qd = np.exp((idx + 1.0)[None, :] * lg[:, None])
    kd = np.where(valid[None, :], np.exp((length - 1.0 - idx)[None, :] * lg[:, None]), 0.0)
    bc = lambda a: np.ascontiguousarray(np.broadcast_to(a[:, :, None], (RET_HEADS, RET_CHUNK, RET_DK)))
    gl = tuple(float(np.exp(length * g)) for g in lg)
    return (jnp.asarray(dm, F32), jnp.asarray(bc(qd), F32), jnp.asarray(bc(kd), F32)), gl


def _retention(rq, rk, rv, state0, length, lb):
    b, s, _ = rq.shape
    (dm, qd, kd), gl = _retention_tables(length)
    seq = lambda w: pl.BlockSpec((1, lb, w), lambda i, c: (i, c, 0))
    st_spec = pl.BlockSpec((1, RET_HEADS, RET_DK, RET_DV), lambda i, c: (i, 0, 0, 0))
    return pl.pallas_call(
        functools.partial(_retention_kernel, n_sub=lb // RET_CHUNK, gl=gl),
        out_shape=(jax.ShapeDtypeStruct((b, s, RET_V_W), BF16),
                   jax.ShapeDtypeStruct((b, RET_HEADS, RET_DK, RET_DV), F32)),
        grid=(b, s // lb),
        in_specs=[seq(RET_QK_W), seq(RET_QK_W), seq(RET_V_W), st_spec,
                  _const_spec(dm.shape), _const_spec(qd.shape), _const_spec(kd.shape)],
        out_specs=(seq(RET_V_W), st_spec),
        scratch_shapes=[pltpu.VMEM((RET_HEADS, RET_DK, RET_DV), F32)],
        compiler_params=_cparams(("parallel", "arbitrary")),
        name="retention",
    )(rq, rk, rv, state0, dm, qd, kd)


def _diff_lambda(lq1_ref, lk1_ref, lq2_ref, lk2_ref, lambda_init):
    a = jnp.sum(lq1_ref[...] * lk1_ref[...], axis=-1, keepdims=True)
    b = jnp.sum(lq2_ref[...] * lk2_ref[...], axis=-1, keepdims=True)
    return jnp.exp(a) - jnp.exp(b) + lambda_init


def _rms_head(o, nw, lambda_init):
    ms = jnp.mean(o * o, axis=-1, keepdims=True)
    return o * lax.rsqrt(ms + LN_EPS) * nw * (1.0 - lambda_init)


def _diffattn_kernel(q_ref, k_ref, v_ref, lq1_ref, lk1_ref, lq2_ref, lk2_ref, nw_ref, o_ref,
                     *, bq, lambda_init):
    qi = pl.program_id(2)
    q = q_ref[0]
    lane = lax.broadcasted_iota(jnp.int32, q.shape, 1)
    zero = jnp.zeros_like(q)
    qq = jnp.concatenate([jnp.where(lane < DIFF_DK, q, zero), jnp.where(lane >= DIFF_DK, q, zero)], axis=0)

    def step(j, carry, masked):
        m, l, acc = carry
        off = pl.multiple_of(j * bq, bq)
        kb = k_ref[0, pl.ds(off, bq), :]
        vb = v_ref[0, pl.ds(off, bq), :]
        s = _dot_nt(qq, kb)
        if masked:
            r = lax.broadcasted_iota(jnp.int32, s.shape, 0)
            r = jnp.where(r >= bq, r - bq, r)
            cidx = lax.broadcasted_iota(jnp.int32, s.shape, 1)
            s = jnp.where(cidx <= r, s, -jnp.inf)
        m_new = jnp.maximum(m, jnp.max(s, axis=-1, keepdims=True))
        p = jnp.exp(s - m_new)
        alpha = jnp.exp(m - m_new)
        l = alpha * l + jnp.sum(p, axis=-1, keepdims=True)
        acc = alpha * acc + _dot(p.astype(BF16), vb)
        return m_new, l, acc

    init = (jnp.full((2 * bq, 1), -jnp.inf, F32), jnp.zeros((2 * bq, 1), F32),
            jnp.zeros((2 * bq, DIFF_DV), F32))
    carry = lax.fori_loop(0, qi, functools.partial(step, masked=False), init)
    _, l, acc = step(qi, carry, True)
    on = acc / l
    lam = _diff_lambda(lq1_ref, lk1_ref, lq2_ref, lk2_ref, lambda_init)
    o = on[:bq] - lam * on[bq:]
    o_ref[0] = _rms_head(o, nw_ref[...], lambda_init).astype(o_ref.dtype)


def _diffattn(dq, dkb, dvb, lam_params, norm_w, lambda_init, bq):
    b, s, _ = dq.shape
    vec = _const_spec((1, DIFF_DK))
    return pl.pallas_call(
        functools.partial(_diffattn_kernel, bq=bq, lambda_init=lambda_init),
        out_shape=jax.ShapeDtypeStruct((b, s, DIFF_V_W), BF16),
        grid=(b, DIFF_HEADS, s // bq),
        in_specs=[pl.BlockSpec((1, bq, 2 * DIFF_DK), lambda i, h, j: (i, j, h)),
                  pl.BlockSpec((1, s, 2 * DIFF_DK), lambda i, h, j: (i, 0, h)),
                  pl.BlockSpec((1, s, DIFF_DV), lambda i, h, j: (i, 0, h)),
                  vec, vec, vec, vec, _const_spec((1, DIFF_DV))],
        out_specs=pl.BlockSpec((1, bq, DIFF_DV), lambda i, h, j: (i, j, h)),
        compiler_params=_cparams(("parallel", "parallel", "arbitrary")),
        name="diffattn",
    )(dq, dkb, dvb, *lam_params, norm_w)


def _decode_kernel(pt_ref, *refs, npg, n_steps, lambda_init):
    k_refs = refs[:npg]
    v_refs = refs[npg:2 * npg]
    (q_ref, kn_ref, vn_ref, lq1_ref, lk1_ref, lq2_ref, lk2_ref, nw_ref,
     o_ref, m_ref, l_ref, acc_ref) = refs[2 * npg:]
    j = pl.program_id(1)

    @pl.when(j == 0)
    def _():
        m_ref[...] = jnp.full(m_ref.shape, -jnp.inf, F32)
        l_ref[...] = jnp.zeros(l_ref.shape, F32)
        acc_ref[...] = jnp.zeros(acc_ref.shape, F32)

    q = q_ref[0]

    def update(kt, pv, mask):
        s = _dot(q, kt)
        if mask is not None:
            s = jnp.where(mask, s, -jnp.inf)
        m = m_ref[...]
        m_new = jnp.maximum(m, jnp.max(s, axis=-1, keepdims=True))
        p = jnp.exp(s - m_new)
        alpha = jnp.exp(m - m_new)
        l_ref[...] = alpha * l_ref[...] + jnp.sum(p, axis=-1, keepdims=True)
        acc_ref[...] = alpha * acc_ref[...] + pv(p.astype(BF16))
        m_ref[...] = m_new

    def pages_pv(p):
        def head_values(h):
            return jnp.concatenate(
                [v_ref[0, pl.ds(h, PAGE_SIZE, stride=DIFF_HEADS), :].astype(BF16) for v_ref in v_refs], axis=0)

        return jnp.concatenate([_dot(p, head_values(h)) for h in range(DIFF_HEADS)], axis=1)

    update(jnp.concatenate([k_ref[0].astype(BF16) for k_ref in k_refs], axis=1), pages_pv, None)

    @pl.when(j == n_steps - 1)
    def _():
        rows = 4 * 8
        r = lax.broadcasted_iota(jnp.int32, (rows, PAGE_SIZE), 0)
        cidx = lax.broadcasted_iota(jnp.int32, (rows, PAGE_SIZE), 1)
        update(kn_ref[0], lambda p: _dot(p, vn_ref[0]), cidx <= (r >> 3))
        lam = _diff_lambda(lq1_ref, lk1_ref, lq2_ref, lk2_ref, lambda_init)
        on = acc_ref[...] / l_ref[...]
        rr = lax.broadcasted_iota(jnp.int32, on.shape, 0)
        cc = lax.broadcasted_iota(jnp.int32, on.shape, 1)
        g = rr & 7
        coef = jnp.where((g & 1) == 0, 1.0, -lam)
        w = jnp.where((cc >> 7) == (g >> 1), coef, 0.0)
        o = jnp.sum((on * w).reshape(4, 8, DIFF_V_W), axis=1)
        for h in range(DIFF_HEADS):
            sl = slice(h * DIFF_DV, (h + 1) * DIFF_DV)
            o_ref[0, :, sl] = _rms_head(o[:, sl], nw_ref[...], lambda_init)


def _decode_attn(page_table, cache_kt, cache_v, qbd, kt_new, v_new, lam_params, norm_w, lambda_init):
    bd, n_pages = page_table.shape
    npg = math.gcd(n_pages, PAGES_PER_STEP)
    n_steps = n_pages // npg

    def page_spec(i):
        return pl.BlockSpec((1, DIFF_QK_W, PAGE_SIZE), lambda b, j, pt: (pt[b, j * npg + i], 0, 0))

    per_seq = lambda r, c: pl.BlockSpec((1, r, c), lambda b, j, pt: (b, 0, 0))
    vec = pl.BlockSpec((1, DIFF_DK), lambda b, j, pt: (0, 0))
    grid_spec = pltpu.PrefetchScalarGridSpec(
        num_scalar_prefetch=1,
        grid=(bd, n_steps),
        in_specs=[page_spec(i) for i in range(npg)] + [page_spec(i) for i in range(npg)]
        + [per_seq(32, DIFF_QK_W), per_seq(DIFF_QK_W, PAGE_SIZE), per_seq(PAGE_SIZE, DIFF_V_W), vec, vec, vec, vec,
           pl.BlockSpec((1, DIFF_DV), lambda b, j, pt: (0, 0))],
        out_specs=per_seq(4, DIFF_V_W),
        scratch_shapes=[pltpu.VMEM((32, 1), F32), pltpu.VMEM((32, 1), F32), pltpu.VMEM((32, DIFF_V_W), F32)],
    )
    return pl.pallas_call(
        functools.partial(_decode_kernel, npg=npg, n_steps=n_steps, lambda_init=lambda_init),
        out_shape=jax.ShapeDtypeStruct((bd, 4, DIFF_V_W), F32),
        grid_spec=grid_spec,
        compiler_params=_cparams(("parallel", "arbitrary")),
        name="decode_attn",
    )(page_table, *([cache_kt] * npg), *([cache_v] * npg), qbd, kt_new, v_new, *lam_params, norm_w)


def _layer_norm(x, w, b):
    mu = jnp.mean(x, axis=-1, keepdims=True)
    xc = x - mu
    var = jnp.mean(xc * xc, axis=-1, keepdims=True)
    return xc * lax.rsqrt(var + LN_EPS) * w + b


def _tail_kernel(x_ref, on_ref, od_ref, wg_ref, rnw_ref, wur_ref, wud_ref, wo_ref, l1w_ref, l1b_ref,
                 sw1_ref, sw3_ref, sw2_ref, h_ref, base_ref, *, alpha):
    x = x_ref[...]
    xb = x.astype(BF16)
    rg = _dot(xb, wg_ref[:, 0:1024])
    o_ret = _silu(rg) * (on_ref[...].astype(F32) * rnw_ref[...])
    u = _sigmoid(_dot(xb, wg_ref[:, 1024:2048])) * _dot(o_ret.astype(BF16), wur_ref[...])
    u = u + _sigmoid(_dot(xb, wg_ref[:, 2048:3072])) * _dot(od_ref[...], wud_ref[...])
    h = _layer_norm(alpha * x + _dot(u.astype(BF16), wo_ref[...]), l1w_ref[...], l1b_ref[...])
    hb = h.astype(BF16)
    a = _silu(_dot(hb, sw1_ref[...])) * _dot(hb, sw3_ref[...])
    _store_slabs(h_ref, _pack_row(h))
    base_ref[...] = alpha * h + _dot(a.astype(BF16), sw2_ref[...])


def _tail(x2d, on, od, w, alpha, tm):
    t = x2d.shape[0]
    row = lambda wd: pl.BlockSpec((tm, wd), lambda i: (i, 0))
    names = ("wg", "ret_norm_w", "w_up_ret", "w_up_diff", "w_out", "ln1_w", "ln1_b", "sw1", "sw3", "sw2")
    ws = [w[n] for n in names]
    return pl.pallas_call(
        functools.partial(_tail_kernel, alpha=alpha),
        out_shape=(jax.ShapeDtypeStruct((t * ROW_SLABS, LANES), jnp.uint32),
                   jax.ShapeDtypeStruct((t, D_MODEL), F32)),
        grid=(t // tm,),
        in_specs=[row(D_MODEL), row(RET_V_W), row(DIFF_V_W)] + [_const_spec(a.shape) for a in ws],
        out_specs=(pl.BlockSpec((tm * ROW_SLABS, LANES), lambda i: (i, 0)), row(D_MODEL)),
        compiler_params=_cparams(("parallel",)),
        name="tail",
    )(x2d, on, od, *ws)


def _first_argmax(x, iota, size):
    m = jnp.max(x, axis=0, keepdims=True)
    idx = jnp.min(jnp.where(x == m, iota, size), axis=0, keepdims=True)
    return m, idx


def _router_kernel(h_ref, rw_ref, rb_ref, eidx_ref, gate_ref, pos_ref, cnt_ref, carry_ref):
    i = pl.program_id(0)
    tb = h_ref.shape[0] // ROW_SLABS
    per = N_EXPERTS // N_GROUPS

    @pl.when(i == 0)
    def _():
        carry_ref[...] = jnp.zeros(carry_ref.shape, F32)

    s = _sigmoid(_dot_nt(rw_ref[...], _unpack_row(_load_slabs(h_ref, tb)).astype(BF16)))
    sb = s + rb_ref[...]
    neg = -jnp.inf

    sb3 = sb.reshape(N_GROUPS, per, tb)
    io3 = lax.broadcasted_iota(jnp.int32, sb3.shape, 1)
    m1 = jnp.max(sb3, axis=1, keepdims=True)
    i1 = jnp.min(jnp.where(sb3 == m1, io3, per), axis=1, keepdims=True)
    m2 = jnp.max(jnp.where(io3 == i1, neg, sb3), axis=1, keepdims=True)
    gscore = (m1 + m2).reshape(N_GROUPS, tb)

    iog = lax.broadcasted_iota(jnp.int32, gscore.shape, 0)
    gsel = jnp.zeros(gscore.shape, F32)
    for _ in range(TOPK_GROUPS):
        _, gi = _first_argmax(gscore, iog, N_GROUPS)
        hit = iog == gi
        gsel = jnp.where(hit, 1.0, gsel)
        gscore = jnp.where(hit, neg, gscore)

    emask = jnp.broadcast_to(gsel.reshape(N_GROUPS, 1, tb), (N_GROUPS, per, tb)).reshape(N_EXPERTS, tb)
    cand = jnp.where(emask > 0.0, sb, neg)
    ioe = lax.broadcasted_iota(jnp.int32, cand.shape, 0)
    chosen = jnp.zeros(cand.shape, F32)
    idxs, gates = [], []
    for _ in range(TOP_K):
        _, ei = _first_argmax(cand, ioe, N_EXPERTS)
        hit = ioe == ei
        gates.append(jnp.sum(jnp.where(hit, s, 0.0), axis=0, keepdims=True))
        idxs.append(ei)
        chosen = jnp.where(hit, 1.0, chosen)
        cand = jnp.where(hit, neg, cand)

    gsum = gates[0]
    for g in gates[1:]:
        gsum = gsum + g
    scale = ROUTED_SCALE / gsum
    for k in range(TOP_K):
        eidx_ref[k:k + 1, :] = idxs[k]
        gate_ref[k:k + 1, :] = gates[k] * scale

    r = lax.broadcasted_iota(jnp.int32, (tb, tb), 0)
    cidx = lax.broadcasted_iota(jnp.int32, (tb, tb), 1)
    before = jnp.where(r < cidx, 1.0, 0.0).astype(BF16)
    rank = carry_ref[...] + _dot(chosen.astype(BF16), before)
    for k in range(TOP_K):
        pk = jnp.sum(jnp.where(ioe == idxs[k], rank, 0.0), axis=0, keepdims=True)
        pos_ref[k:k + 1, :] = pk.astype(jnp.int32)
    total = carry_ref[...] + jnp.sum(chosen, axis=1, keepdims=True)
    carry_ref[...] = total
    cnt_ref[...] = total.astype(jnp.int32)


def _router(h, rw_t, rb_col, tb):
    t = h.shape[0] // ROW_SLABS
    slot = pl.BlockSpec((TOP_K, tb), lambda i: (0, i))
    return pl.pallas_call(
        _router_kernel,
        out_shape=(jax.ShapeDtypeStruct((TOP_K, t), jnp.int32), jax.ShapeDtypeStruct((TOP_K, t), F32),
                   jax.ShapeDtypeStruct((TOP_K, t), jnp.int32), jax.ShapeDtypeStruct((N_EXPERTS, 1), jnp.int32)),
        grid=(t // tb,),
        in_specs=[pl.BlockSpec((tb * ROW_SLABS, LANES), lambda i: (i, 0)), _const_spec(rw_t.shape),
                  _const_spec(rb_col.shape)],
        out_specs=(slot, slot, slot, _const_spec((N_EXPERTS, 1))),
        scratch_shapes=[pltpu.VMEM((N_EXPERTS, 1), F32)],
        compiler_params=_cparams(("arbitrary",)),
        name="router",
    )(h, rw_t, rb_col)


def _row_copy_wait(src_rows, dst_rows, sem):
    pltpu.make_async_copy(src_rows, dst_rows, sem).wait()


_PAD_CHUNKS = tuple(1 << s for s in reversed(range(MOE_BLOCK.bit_length() - 1)))


def _zero_fill_padding(ps_ref, cnt_ref, zero_ref, xs_ref, sem):
    def chunks(e, act):
        cnt = cnt_ref[e]
        n_pad = (MOE_BLOCK - (cnt & (MOE_BLOCK - 1))) & (MOE_BLOCK - 1)
        off = ps_ref[e] + cnt
        for rows in _PAD_CHUNKS:
            take = n_pad & rows

            @pl.when(take != 0)
            def _(off=off, rows=rows):
                act(pltpu.make_async_copy(zero_ref.at[pl.ds(0, rows)], xs_ref.at[pl.ds(off, rows)], sem))

            off = off + take

    def start(e, carry):
        chunks(e, lambda cp: cp.start())
        return carry

    def wait(e, carry):
        chunks(e, lambda cp: cp.wait())
        return carry

    lax.fori_loop(0, N_EXPERTS, start, 0)
    lax.fori_loop(0, N_EXPERTS, wait, 0)


def _zero_fill_unused_blocks(nb_ref, zero_ref, xs_ref, sem):
    rows = zero_ref.shape[0]
    n_total = xs_ref.shape[0] // MOE_BLOCK

    def copies(blk, act):
        for j in range(MOE_BLOCK // rows):
            act(pltpu.make_async_copy(zero_ref, xs_ref.at[pl.ds(blk * MOE_BLOCK + j * rows, rows)], sem))

    def start(blk, carry):
        copies(blk, lambda cp: cp.start())
        return carry

    def wait(blk, carry):
        copies(blk, lambda cp: cp.wait())
        return carry

    lax.fori_loop(nb_ref[0], n_total, start, 0)
    lax.fori_loop(nb_ref[0], n_total, wait, 0)


def _dest_kernel(ps_ref, eidx_ref, pos_ref, dest_ref):
    ps = ps_ref[...]
    ioe = lax.broadcasted_iota(jnp.int32, (N_EXPERTS, eidx_ref.shape[1]), 0)
    for k in range(TOP_K):
        start = jnp.sum(jnp.where(ioe == eidx_ref[k:k + 1, :], ps, 0.0), axis=0, keepdims=True)
        dest_ref[k:k + 1, :] = start.astype(jnp.int32) + pos_ref[k:k + 1, :]


def _dest_rows(pad_start, eidx, pos, tb):
    t = eidx.shape[1]
    slot = pl.BlockSpec((TOP_K, tb), lambda i: (0, i))
    return pl.pallas_call(
        _dest_kernel,
        out_shape=jax.ShapeDtypeStruct((TOP_K, t), jnp.int32),
        grid=(t // tb,),
        in_specs=[_const_spec((N_EXPERTS, 1)), slot, slot],
        out_specs=slot,
        compiler_params=_cparams(("parallel",)),
        name="dest_rows",
    )(pad_start.astype(F32).reshape(N_EXPERTS, 1), eidx, pos)


def _dispatch_kernel(ps_ref, cnt_ref, nb_ref, dest_ref, h_ref, xs_ref, zero_ref, sem, zsem):
    per_tile = SUBLANES // ROW_SLABS
    tm = h_ref.shape[0] // ROW_SLABS

    @pl.when(pl.program_id(0) == 0)
    def _():
        zero_ref[...] = jnp.zeros(zero_ref.shape, zero_ref.dtype)
        _zero_fill_padding(ps_ref, cnt_ref, zero_ref, xs_ref, zsem)
        _zero_fill_unused_blocks(nb_ref, zero_ref, xs_ref, zsem)

    def body(u, carry):
        for par in range(per_tile):
            t = u * per_tile + par
            for k in range(TOP_K):
                pltpu.make_async_copy(_slab(h_ref, u, par), xs_ref.at[dest_ref[k, t]], sem).start(priority=k % 2)
        return carry

    lax.fori_loop(0, tm // per_tile, body, 0)
    for k in range(TOP_K):
        _row_copy_wait(h_ref, h_ref, sem)


def _dispatch(pad_start, counts, n_used, dest, h, n_rows, tm):
    t = h.shape[0] // ROW_SLABS
    slot = pl.BlockSpec((TOP_K, tm), lambda i, ps, cnt, nb: (0, i), memory_space=pltpu.SMEM)
    grid_spec = pltpu.PrefetchScalarGridSpec(
        num_scalar_prefetch=3,
        grid=(t // tm,),
        in_specs=[slot, pl.BlockSpec((tm * ROW_SLABS, LANES), lambda i, ps, cnt, nb: (i, 0))],
        out_specs=pl.BlockSpec(memory_space=pl.ANY),
        scratch_shapes=[pltpu.VMEM((_PAD_CHUNKS[0], ROW_SLABS, LANES), jnp.uint32), pltpu.SemaphoreType.DMA(()),
                        pltpu.SemaphoreType.DMA(())],
    )
    return pl.pallas_call(
        _dispatch_kernel,
        out_shape=jax.ShapeDtypeStruct((n_rows, ROW_SLABS, LANES), jnp.uint32),
        grid_spec=grid_spec,
        compiler_params=_cparams(("arbitrary",)),
        name="dispatch",
    )(pad_start, counts, n_used, dest, h)


def _expert_kernel(be_ref, nb_ref, xs_ref, w1_ref, w3_ref, w2_ref, ys_ref, w1b_ref, w3b_ref, w2b_ref):
    i = pl.program_id(0)

    @pl.when(jnp.logical_or(i == 0, be_ref[i] != be_ref[jnp.maximum(i - 1, 0)]))
    def _():
        w1b_ref[...] = w1_ref[0].astype(BF16)
        w3b_ref[...] = w3_ref[0].astype(BF16)
        w2b_ref[...] = w2_ref[0].astype(BF16)

    @pl.when(i < nb_ref[0])
    def _():
        xb = _unpack_row(_load_slabs(xs_ref, MOE_BLOCK)).astype(BF16)
        a = _silu(_dot(xb, w1b_ref[...])) * _dot(xb, w3b_ref[...])
        _store_slabs(ys_ref, _pack_row(_dot(a.astype(BF16), w2b_ref[...])))

    @pl.when(i >= nb_ref[0])
    def _():
        ys_ref[...] = jnp.zeros(ys_ref.shape, ys_ref.dtype)


def _experts(block_expert, n_used, xs, w1, w3, w2):
    n_rows = xs.shape[0] // ROW_SLABS
    grid_spec = pltpu.PrefetchScalarGridSpec(
        num_scalar_prefetch=2,
        grid=(n_rows // MOE_BLOCK,),
        in_specs=[pl.BlockSpec((MOE_BLOCK * ROW_SLABS, LANES), lambda i, be, nb: (jnp.minimum(i, nb[0] - 1), 0)),
                  pl.BlockSpec((1, D_MODEL, D_EXPERT), lambda i, be, nb: (be[i], 0, 0)),
                  pl.BlockSpec((1, D_MODEL, D_EXPERT), lambda i, be, nb: (be[i], 0, 0)),
                  pl.BlockSpec((1, D_EXPERT, D_MODEL), lambda i, be, nb: (be[i], 0, 0))],
        out_specs=pl.BlockSpec((MOE_BLOCK * ROW_SLABS, LANES), lambda i, be, nb: (i, 0)),
        scratch_shapes=[pltpu.VMEM((D_MODEL, D_EXPERT), BF16), pltpu.VMEM((D_MODEL, D_EXPERT), BF16),
                        pltpu.VMEM((D_EXPERT, D_MODEL), BF16)],
    )
    return pl.pallas_call(
        _expert_kernel,
        out_shape=jax.ShapeDtypeStruct((n_rows * ROW_SLABS, LANES), jnp.uint32),
        grid_spec=grid_spec,
        compiler_params=_cparams(("arbitrary",)),
        name="experts",
    )(block_expert, n_used, xs, w1, w3, w2)


def _combine_kernel(dcur_ref, dnxt_ref, base_ref, gate_ref, p_ref, ys_ref,
                    l2w_ref, l2b_ref, wpg_ref, wpe_ref, y_ref, rows_ref, sems):
    i = pl.program_id(0)
    n = pl.num_programs(0)
    tm = base_ref.shape[0]

    per_tile = SUBLANES // ROW_SLABS

    def start_gather(d_ref, slot):
        def body(u, carry):
            for par in range(per_tile):
                t = u * per_tile + par
                for k in range(TOP_K):
                    pltpu.make_async_copy(ys_ref.at[d_ref[k, t]], _slab(rows_ref.at[slot, k], u, par),
                                          sems.at[slot]).start(priority=k % 2)
            return carry

        lax.fori_loop(0, tm // per_tile, body, 0)

    @pl.when(i == 0)
    def _():
        start_gather(dcur_ref, 0)

    slot = i % 2
    n_pairs = tm // per_tile

    def start_gather_piece(d_ref, s, piece):
        for u in range(piece * n_pairs // TOP_K, (piece + 1) * n_pairs // TOP_K):
            for par in range(per_tile):
                lo = u * SUBLANES + par * ROW_SLABS
                for k in range(TOP_K):
                    pltpu.make_async_copy(ys_ref.at[d_ref[k, u * per_tile + par]],
                                          rows_ref.at[s, k, lo:lo + ROW_SLABS], sems.at[s]).start(priority=k % 2)

    def wait_rows(s):
        for k in range(TOP_K):
            _row_copy_wait(rows_ref.at[s, k], rows_ref.at[s, k], sems.at[s])

    def reduce_and_finish(s, issue_piece):
        g = gate_ref[...]
        m = base_ref[...]
        for k in range(TOP_K):
            issue_piece(k)
            m = m + _unpack_row(_load_slabs(rows_ref.at[s, k], tm)) * g[:, k:k + 1]
        h2 = _layer_norm(m, l2w_ref[...], l2b_ref[...])
        emb = _dot(p_ref[...].astype(BF16), wpe_ref[...])
        y_ref[...] = h2 + _sigmoid(_dot(h2.astype(BF16), wpg_ref[...])) * emb

    for s in range(2):
        @pl.when(jnp.logical_and(slot == s, i + 1 < n))
        def _(s=s):
            wait_rows(s)
            reduce_and_finish(s, lambda piece: start_gather_piece(dnxt_ref, 1 - s, piece))

        @pl.when(jnp.logical_and(slot == s, i + 1 == n))
        def _(s=s):
            wait_rows(s)
            reduce_and_finish(s, lambda piece: None)


def _combine(dest, base, gate_t, p2d, ys, w, tm):
    t = base.shape[0]
    n = t // tm
    row = lambda wd: pl.BlockSpec((tm, wd), lambda i: (i, 0))
    cur = pl.BlockSpec((TOP_K, tm), lambda i: (0, i), memory_space=pltpu.SMEM)
    nxt = pl.BlockSpec((TOP_K, tm), lambda i: (0, jnp.minimum(i + 1, n - 1)), memory_space=pltpu.SMEM)
    ws = [w[nm] for nm in ("ln2_w", "ln2_b", "w_pg", "w_pe")]
    return pl.pallas_call(
        _combine_kernel,
        out_shape=jax.ShapeDtypeStruct((t, D_MODEL), F32),
        grid=(n,),
        in_specs=[cur, nxt, row(D_MODEL), row(TOP_K), row(p2d.shape[1]), pl.BlockSpec(memory_space=pl.ANY)]
        + [_const_spec(a.shape) for a in ws],
        out_specs=row(D_MODEL),
        scratch_shapes=[pltpu.VMEM((2, TOP_K, tm * ROW_SLABS, LANES), jnp.uint32), pltpu.SemaphoreType.DMA((2,))],
        compiler_params=_cparams(("arbitrary",)),
        name="combine",
    )(dest, dest, base, gate_t, p2d, ys, *ws)


def _moe_and_out(h, base, p2d, w, tb, tm_disp, tm_comb):
    t = h.shape[0] // ROW_SLABS
    eidx, gate, pos, counts = _router(h, w["router_wt"], w["router_b"], tb)
    counts = counts[:, 0]
    padded = (counts + MOE_BLOCK - 1) // MOE_BLOCK * MOE_BLOCK
    pad_end = jnp.cumsum(padded)
    pad_start = pad_end - padded
    n_blocks = (t * TOP_K + N_EXPERTS * (MOE_BLOCK - 1) + MOE_BLOCK - 1) // MOE_BLOCK
    block_start = jnp.arange(n_blocks, dtype=jnp.int32) * MOE_BLOCK
    block_expert = jnp.minimum(jnp.sum((pad_end[None, :] <= block_start[:, None]).astype(jnp.int32), axis=1),
                               N_EXPERTS - 1)
    n_used = (pad_end[-1:] // MOE_BLOCK).astype(jnp.int32)
    dest = _dest_rows(pad_start, eidx, pos, tb)
    xs = _dispatch(pad_start, counts, n_used, dest, h, n_blocks * MOE_BLOCK, tm_disp)
    ys = _experts(block_expert, n_used, xs.reshape(-1, LANES), w["ew1"], w["ew3"], w["ew2"])
    return _combine(dest, base, gate.T, p2d, ys.reshape(-1, ROW_SLABS, LANES), w, tm_comb)


def _layer_weights(i, w_in, ret_norm_w, diff_norm_w, w_up_ret, w_up_diff, w_out, ln1_w, ln1_b, router_w, router_b,
                   expert_w1, expert_w3, expert_w2, shared_w1, shared_w3, shared_w2, ln2_w, ln2_b, w_pe, w_pg):
    o = _OFFS
    wi = w_in[i]
    vec = lambda a: a[i].reshape(1, -1).astype(F32)
    return {
        "w_mix": jnp.concatenate([wi[:, o[0]:o[3]], wi[:, o[4]:o[7]]], axis=1).astype(BF16),
        "wg": jnp.concatenate([wi[:, o[3]:o[4]], wi[:, o[7]:o[9]]], axis=1).astype(BF16),
        "ret_norm_w": vec(ret_norm_w), "diff_norm_w": vec(diff_norm_w),
        "w_up_ret": w_up_ret[i].astype(BF16), "w_up_diff": w_up_diff[i].astype(BF16), "w_out": w_out[i].astype(BF16),
        "ln1_w": vec(ln1_w), "ln1_b": vec(ln1_b), "ln2_w": vec(ln2_w), "ln2_b": vec(ln2_b),
        "router_wt": router_w[i].T.astype(BF16), "router_b": router_b[i].reshape(-1, 1).astype(F32),
        "ew1": expert_w1[i], "ew3": expert_w3[i], "ew2": expert_w2[i],
        "sw1": shared_w1[i].astype(BF16), "sw3": shared_w3[i].astype(BF16), "sw2": shared_w2[i].astype(BF16),
        "w_pe": w_pe[i].astype(BF16), "w_pg": w_pg[i].astype(BF16),
    }


def _pick(n, pref):
    return pref if n % pref == 0 else n


def kernel(x_prompt, x_sample, cache_diff_k, cache_diff_v, state_retention, page_table, p_prompt, p_sample, w_in, ret_norm_w, diff_lq1, diff_lk1, diff_lq2, diff_lk2, diff_norm_w, w_up_ret, w_up_diff, w_out, ln1_w, ln1_b, router_w, router_b, expert_w1, expert_w3, expert_w2, shared_w1, shared_w3, shared_w2, ln2_w, ln2_b, w_pe, w_pg):
    depth = w_in.shape[0]
    b, s, d = x_prompt.shape
    bd, ls, _ = x_sample.shape
    n_pages = page_table.shape[1]
    past = n_pages * PAGE_SIZE
    alpha = (2 * depth) ** 0.25
    tp, ts = b * s, bd * ls

    cos_p, sin_p = _rotation_tables(jnp.arange(s))
    cos_s, sin_s = _rotation_tables(past + jnp.arange(ls))
    cos_s = jnp.tile(cos_s, (bd, 1))
    sin_s = jnp.tile(sin_s, (bd, 1))

    yp, ys = x_prompt.reshape(tp, d), x_sample.reshape(ts, d)
    kp_l, vp_l, sp_l, ks_l, vs_l, ss_l = [], [], [], [], [], []
    for i in range(depth):
        lambda_init = 0.8 - 0.6 * math.exp(-0.3 * i)
        w = _layer_weights(i, w_in, ret_norm_w, diff_norm_w, w_up_ret, w_up_diff, w_out, ln1_w, ln1_b, router_w,
                           router_b, expert_w1, expert_w3, expert_w2, shared_w1, shared_w3, shared_w2,
                           ln2_w, ln2_b, w_pe, w_pg)
        lam_params = tuple(a[i].reshape(1, DIFF_DK).astype(F32) for a in (diff_lq1, diff_lk1, diff_lq2, diff_lk2))

        rq, rk, rv, dq, dk, dv, dkb, dvb = _proj(yp, w["w_mix"], cos_p, sin_p, _pick(s, 512))
        seq = lambda a: a.reshape(b, s, a.shape[-1])
        on, st_p = _retention(seq(rq), seq(rk), seq(rv), jnp.zeros((b, RET_HEADS, RET_DK, RET_DV), F32),
                              RET_CHUNK, _pick(s, 512))
        od = _diffattn(seq(dq), seq(dkb), seq(dvb), lam_params, w["diff_norm_w"], lambda_init, _pick(s, 512))
        h, base = _tail(yp, on.reshape(tp, RET_V_W), od.reshape(tp, DIFF_V_W), w, alpha, _pick(tp, 256))
        yp = _moe_and_out(h, base, p_prompt[i].reshape(tp, -1), w, _pick(tp, 512), _pick(tp, 512), _pick(tp, 128))
        kp_l.append(dk.reshape(b, s, DIFF_HEADS, 2, DIFF_DK))
        vp_l.append(dv.reshape(b, s, DIFF_HEADS, DIFF_DV))
        sp_l.append(st_p)

        rq, rk, rv, dq, dk, dv, dkb, dvb = _proj(ys, w["w_mix"], cos_s, sin_s, _pick(ts, 512))
        pad_rows = lambda a, r: jnp.pad(a.reshape(bd, ls, a.shape[-1]), ((0, 0), (0, r - ls), (0, 0)))
        on, st_s = _retention(pad_rows(rq, RET_CHUNK), pad_rows(rk, RET_CHUNK), pad_rows(rv, RET_CHUNK),
                              state_retention[i], ls, RET_CHUNK)
        on = on[:, :ls].reshape(ts, RET_V_W)
        q5 = dq.reshape(bd, ls, 1, 2 * DIFF_HEADS, DIFF_DK)
        eye = jnp.eye(2 * DIFF_HEADS, dtype=BF16)[None, None, :, :, None]
        qbd = (q5 * eye).reshape(bd, ls * 2 * DIFF_HEADS, DIFF_QK_W)
        cache_kt = jnp.transpose(cache_diff_k[i], (0, 2, 3, 4, 1)).reshape(-1, DIFF_QK_W, PAGE_SIZE)
        cache_v = cache_diff_v[i].reshape(-1, PAGE_SIZE * DIFF_HEADS, DIFF_DV)
        kt_new = jnp.pad(jnp.swapaxes(dkb.reshape(bd, ls, DIFF_QK_W), 1, 2), ((0, 0), (0, 0), (0, PAGE_SIZE - ls)))
        od = _decode_attn(page_table, cache_kt, cache_v, qbd, kt_new, pad_rows(dvb, PAGE_SIZE), lam_params,
                          w["diff_norm_w"], lambda_init)
        od = od.reshape(ts, DIFF_V_W).astype(BF16)
        h, base = _tail(ys, on, od, w, alpha, _pick(ts, 256))
        ys = _moe_and_out(h, base, p_sample[i].reshape(ts, -1), w, _pick(ts, 512), _pick(ts, 256), _pick(ts, 128))
        ks_l.append(dk.reshape(bd, ls, DIFF_HEADS, 2, DIFF_DK))
        vs_l.append(dv.reshape(bd, ls, DIFF_HEADS, DIFF_DV))
        ss_l.append(st_s)

    return (yp.reshape(b, s, d), ys.reshape(bd, ls, d), jnp.stack(kp_l), jnp.stack(vp_l), jnp.stack(sp_l),
            jnp.stack(ks_l), jnp.stack(vs_l), jnp.stack(ss_l))
```
